```python
import math
import jax, jax.numpy as jnp
from jax import lax
import numpy as np

D_MODEL = 1024
BATCH = 8
SEQ = 2048
DEPTH = 2

HEAD_DIM = 64
SWA_HEADS = (D_MODEL // 2) // HEAD_DIM
SWA_KV_HEADS = SWA_HEADS // 4
SWA_GROUP = SWA_HEADS // SWA_KV_HEADS
WINDOW = 128
SWA_BLOCK = 128
SB_HEADS = (D_MODEL // 2) // HEAD_DIM
SB_BLOCK = 128
HG_DK = 128
HG_WIDTH = D_MODEL // 2
HG_HEADS = HG_WIDTH // HG_DK
HG_DV = HG_WIDTH // HG_HEADS
HG_CHUNK = 64
CONV_WIDTH = D_MODEL // 2
CONV_K = 3
D_FF = 2816
EPS = 1e-6

N_EVEN = (DEPTH + 1) // 2
N_ODD = DEPTH // 2
SWA_Q = SWA_HEADS * HEAD_DIM
SWA_KV = SWA_KV_HEADS * HEAD_DIM
SB_W = SB_HEADS * HEAD_DIM
ATTN_IN = SWA_Q + 2 * SWA_KV + 3 * SB_W
ATTN_OUT = SWA_Q + SB_W
REC_IN = 4 * HG_WIDTH + 3 * CONV_WIDTH
REC_OUT = HG_HEADS * HG_DV + CONV_WIDTH

kernel_name = "hybrid_swa_stickbreak_hgrn2_shortconv_macaron"

F32 = jnp.float32


def split_cols(t, sizes):
    return jnp.split(t, np.cumsum(sizes)[:-1].tolist(), axis=-1)


def rms_norm(x, g):
    x32 = x.astype(F32)
    y = x32 * lax.rsqrt(jnp.mean(x32 * x32, axis=-1, keepdims=True) + EPS)
    return (y * g.astype(F32)).astype(x.dtype)


def swiglu(x, w_in, w_out):
    gate, up = jnp.split(x @ w_in, 2, axis=-1)
    return (jax.nn.silu(gate) * up) @ w_out


def alibi_slopes(n_heads):
    return jnp.exp2(-8.0 * jnp.arange(1, n_heads + 1, dtype=F32) / n_heads)


def sliding_window_attention(q, k, v, sinks):
    b, s, _, dh = q.shape
    nb = s // SWA_BLOCK
    qb = q.reshape(b, nb, SWA_BLOCK, SWA_KV_HEADS, SWA_GROUP, dh)

    def band(t):
        tb = t.reshape(b, nb, SWA_BLOCK, SWA_KV_HEADS, dh)
        prev = jnp.concatenate([jnp.zeros_like(tb[:, :1]), tb[:, :-1]], axis=1)
        return jnp.concatenate([prev, tb], axis=2)

    kb, vb = band(k), band(v)
    scores = jnp.einsum('bnqhgd,bnkhd->bnhgqk', qb, kb).astype(F32) * (dh ** -0.5)
    qi = jnp.arange(SWA_BLOCK)[:, None]
    kj = jnp.arange(2 * SWA_BLOCK)[None, :]
    dist = qi + SWA_BLOCK - kj
    key_pos = jnp.arange(nb)[:, None, None] * SWA_BLOCK - SWA_BLOCK + kj
    valid = (dist >= 0) & (dist < WINDOW) & (key_pos >= 0)
    slopes = alibi_slopes(SWA_HEADS).reshape(SWA_KV_HEADS, SWA_GROUP)
    scores = scores - slopes[:, :, None, None] * dist.astype(F32)
    scores = jnp.where(valid[None, :, None, None], scores, -jnp.inf)
    sink = sinks.astype(F32).reshape(SWA_KV_HEADS, SWA_GROUP)[None, None, :, :, None, None]
    m = jnp.maximum(jnp.max(scores, axis=-1, keepdims=True), sink)
    p = jnp.exp(scores - m)
    probs = p / (jnp.sum(p, axis=-1, keepdims=True) + jnp.exp(sink - m))
    o = jnp.einsum('bnhgqk,bnkhd->bnqhgd', probs.astype(v.dtype), vb)
    return o.reshape(b, s, SWA_HEADS * dh)


def stick_breaking_attention(q, k, v):
    b, s, h, dh = q.shape
    outs = []
    for n in range(s // SB_BLOCK):
        start, end = n * SB_BLOCK, (n + 1) * SB_BLOCK
        z = jnp.einsum('bqhd,bkhd->bhqk', q[:, start:end], k[:, :end]).astype(F32) * (dh ** -0.5)
        t_pos = start + jnp.arange(SB_BLOCK)[:, None]
        s_pos = jnp.arange(end)[None, :]
        causal = s_pos < t_pos
        log_keep = jnp.where(causal, -jax.nn.softplus(z), 0.0)
        suffix = lax.cumsum(log_keep, axis=3, reverse=True)
        between = jnp.concatenate([suffix[..., 1:], jnp.zeros_like(suffix[..., :1])], axis=-1)
        w = jnp.where(causal, jnp.exp(jax.nn.log_sigmoid(z) + between), 0.0)
        outs.append(jnp.einsum('bhqk,bkhd->bqhd', w.astype(v.dtype), v[:, :end]))
    return jnp.concatenate(outs, axis=1).reshape(b, s, h * dh)


def hgrn_lower_bounds(logits):
    c = jnp.cumsum(jax.nn.softmax(logits.astype(F32), axis=0), axis=0)
    return c - c[0:1]


def hgrn2(q, f_logit, i, lb):
    b, s, h, dk = q.shape
    nc = s // HG_CHUNK
    z = f_logit.astype(F32)
    lb = lb.astype(F32)
    log_f = jnp.log(lb + (1.0 - lb) * jax.nn.sigmoid(z))
    key = (1.0 - lb) * jax.nn.sigmoid(-z)
    qf = jax.nn.silu(q.astype(F32))
    vf = i.astype(F32)

    def chunks(t):
        return t.reshape(b, nc, HG_CHUNK, h, t.shape[-1]).transpose(1, 0, 3, 2, 4)

    qc, kc, vc = chunks(qf), chunks(key), chunks(vf)
    gc = lax.cumsum(chunks(log_f), axis=3)
    causal = jnp.tril(jnp.ones((HG_CHUNK, HG_CHUNK), dtype=bool))[:, :, None]

    def step(state, inp):
        qt, kt, vt, gt = inp
        o_inter = jnp.einsum('bhtd,bhde->bhte', qt * jnp.exp(gt), state)
        diff = gt[:, :, :, None, :] - gt[:, :, None, :, :]
        decay = jnp.exp(jnp.where(causal, diff, -jnp.inf))
        scores = jnp.einsum('bhtd,bhsd,bhtsd->bhts', qt, kt, decay)
        o_intra = jnp.einsum('bhts,bhse->bhte', scores, vt)
        g_last = gt[:, :, -1:, :]
        state = (jnp.exp(g_last)[:, :, 0, :, None] * state
                 + jnp.einsum('bhsd,bhse->bhde', kt * jnp.exp(g_last - gt), vt))
        return state, o_inter + o_intra

    init = jnp.zeros((b, h, dk, vf.shape[-1]), F32)
    _, o = lax.scan(step, init, (qc, kc, vc, gc))
    return o.transpose(1, 0, 3, 2, 4).reshape(b, s, h, vf.shape[-1])


def short_conv(u, w):
    return lax.conv_general_dilated(u, w[:, None, :].astype(u.dtype), window_strides=(1,),
                                    padding=[(CONV_K - 1, 0)],
                                    dimension_numbers=('NWC', 'WIO', 'NWC'),
                                    feature_group_count=u.shape[-1])


def attention_mixer(h, w_in, sinks, w_out):
    b, s, _ = h.shape
    q_a, k_a, v_a, q_b, k_b, v_b = split_cols(h @ w_in, (SWA_Q, SWA_KV, SWA_KV, SB_W, SB_W, SB_W))
    o_a = sliding_window_attention(q_a.reshape(b, s, SWA_HEADS, HEAD_DIM),
                                   k_a.reshape(b, s, SWA_KV_HEADS, HEAD_DIM),
                                   v_a.reshape(b, s, SWA_KV_HEADS, HEAD_DIM), sinks)
    o_b = stick_breaking_attention(q_b.reshape(b, s, SB_HEADS, HEAD_DIM),
                                   k_b.reshape(b, s, SB_HEADS, HEAD_DIM),
                                   v_b.reshape(b, s, SB_HEADS, HEAD_DIM))
    return jnp.concatenate([o_a.astype(h.dtype), o_b.astype(h.dtype)], axis=-1) @ w_out


def recurrent_conv_mixer(h, w_in, lb, out_norm_g, conv_w, w_out):
    b, s, _ = h.shape
    q, f, i, g, gate_b, gate_c, u = split_cols(
        h @ w_in, (HG_WIDTH, HG_WIDTH, HG_WIDTH, HG_WIDTH, CONV_WIDTH, CONV_WIDTH, CONV_WIDTH))
    hs = (b, s, HG_HEADS, HG_DK)
    o_c = hgrn2(q.reshape(hs), f.reshape(hs), i.reshape(b, s, HG_HEADS, HG_DV),
                lb.reshape(HG_HEADS, HG_DK))
    o_c = rms_norm(o_c, out_norm_g).reshape(b, s, HG_HEADS * HG_DV) * jax.nn.silu(g.astype(F32))
    o_d = gate_b * short_conv(gate_c * u, conv_w)
    return jnp.concatenate([o_c.astype(h.dtype), o_d.astype(h.dtype)], axis=-1) @ w_out


def setup_inputs(seed: int = 0) -> dict:
    key = jax.random.key(seed)
    ks = jax.random.split(key, 13)

    def w(k, shape, fan_in):
        return jax.random.normal(k, shape, F32) * (fan_in ** -0.5)

    def gain(k, shape):
        return 1.0 + 0.02 * jax.random.normal(k, shape, F32)

    return {
        "x": jax.random.normal(ks[0], (BATCH, SEQ, D_MODEL), F32),
        "norm_g": gain(ks[1], (DEPTH, 3, D_MODEL)),
        "ffn_w_in": w(ks[2], (DEPTH, 2, D_MODEL, 2 * D_FF), D_MODEL),
        "ffn_w_out": w(ks[3], (DEPTH, 2, D_FF, D_MODEL), D_FF),
        "attn_w_in": w(ks[4], (N_EVEN, D_MODEL, ATTN_IN), D_MODEL),
        "attn_sinks": 0.5 * jax.random.normal(ks[5], (N_EVEN, SWA_HEADS), F32),
        "attn_w_out": w(ks[6], (N_EVEN, ATTN_OUT, D_MODEL), ATTN_OUT),
        "rec_w_in": w(ks[7], (N_ODD, D_MODEL, REC_IN), D_MODEL),
        "hgrn_lb_logits": jax.random.normal(ks[8], (DEPTH, HG_WIDTH), F32),
        "hgrn_norm_g": gain(ks[9], (N_ODD, HG_DV)),
        "conv_w": w(ks[10], (N_ODD, CONV_K, CONV_WIDTH), CONV_K),
        "rec_w_out": w(ks[11], (N_ODD, REC_OUT, D_MODEL), REC_OUT),
        "final_g": gain(ks[12], (D_MODEL,)),
    }


def reference(x, norm_g, ffn_w_in, ffn_w_out, attn_w_in, attn_sinks, attn_w_out,
              rec_w_in, hgrn_lb_logits, hgrn_norm_g, conv_w, rec_w_out, final_g):
    lower_bounds = hgrn_lower_bounds(hgrn_lb_logits)
    h = x
    for layer in range(DEPTH):
        h = h + 0.5 * swiglu(rms_norm(h, norm_g[layer, 0]), ffn_w_in[layer, 0], ffn_w_out[layer, 0])
        hn = rms_norm(h, norm_g[layer, 1])
        if layer % 2 == 0:
            e = layer // 2
            mix = attention_mixer(hn, attn_w_in[e], attn_sinks[e], attn_w_out[e])
        else:
            o = layer // 2
            mix = recurrent_conv_mixer(hn, rec_w_in[o], lower_bounds[layer], hgrn_norm_g[o],
                                       conv_w[o], rec_w_out[o])
        h = h + mix
        h = h + 0.5 * swiglu(rms_norm(h, norm_g[layer, 2]), ffn_w_in[layer, 1], ffn_w_out[layer, 1])
    return rms_norm(h, final_g)
```

```python
import functools
import math

import jax
import jax.numpy as jnp
from jax import lax
from jax.experimental import pallas as pl
from jax.experimental.pallas import tpu as pltpu

F32 = jnp.float32
BF16 = jnp.bfloat16

D_MODEL = 1024
HEAD_DIM = 64
SWA_HEADS = 8
SWA_KV_HEADS = 2
SWA_GROUP = SWA_HEADS // SWA_KV_HEADS
WINDOW = 128
BLK = 128
SB_HEADS = 8
HG_WIDTH = 512
HG_DK = 128
HG_HEADS = HG_WIDTH // HG_DK
HG_CHUNK = 64
HG_SUB = 16
CONV_WIDTH = 512
CONV_K = 3
D_FF = 2816
EPS = 1e-6

SWA_Q = SWA_HEADS * HEAD_DIM
SWA_KV = SWA_KV_HEADS * HEAD_DIM
SB_W = SB_HEADS * HEAD_DIM
ATTN_IN = SWA_Q + 2 * SWA_KV + 3 * SB_W
REC_IN = 4 * HG_WIDTH + 3 * CONV_WIDTH

VMEM_LIMIT_BYTES = 56 * 1024 * 1024


def _params(semantics):
    return pltpu.CompilerParams(dimension_semantics=semantics,
                                vmem_limit_bytes=VMEM_LIMIT_BYTES)


def _rms(x, g):
    return x * lax.rsqrt(jnp.mean(x * x, axis=-1, keepdims=True) + EPS) * g


def _dot(a, b):
    return jnp.dot(a, b, preferred_element_type=F32)


def _dot_nt(a, b):
    return lax.dot_general(a, b, (((1,), (1,)), ((), ())), preferred_element_type=F32)


def _dot_tn(a, b):
    return lax.dot_general(a, b, (((0,), (0,)), ((), ())), preferred_element_type=F32)


def _split_dot(x, m01):
    hi = x.astype(BF16)
    lo = (x - hi.astype(F32)).astype(BF16)
    return _dot(hi, m01) + _dot(lo, m01)


def _split_dot_left(m01, x):
    hi = x.astype(BF16)
    lo = (x - hi.astype(F32)).astype(BF16)
    return _dot(m01, hi) + _dot(m01, lo)


def _ffn_kernel(x_ref, g_ref, wg_ref, wu_ref, wo_ref, fg_ref, o_ref, n_scr, acc_scr, *, final_norm):
    f = pl.program_id(1)

    @pl.when(f == 0)
    def _():
        n_scr[...] = _rms(x_ref[...], g_ref[...]).astype(BF16)
        acc_scr[...] = jnp.zeros_like(acc_scr)

    n = n_scr[...]
    gate = _dot(n, wg_ref[...])
    up = _dot(n, wu_ref[...])
    act = (gate * jax.nn.sigmoid(gate) * up).astype(BF16)
    acc_scr[...] += _dot(act, wo_ref[...])

    @pl.when(f == pl.num_programs(1) - 1)
    def _():
        h = x_ref[...] + 0.5 * acc_scr[...]
        if final_norm:
            h = _rms(h, fg_ref[...])
        o_ref[...] = h


def _ffn(h, g, w_in, w_out, final_g, *, final_norm):
    m, d = h.shape
    d_ff = w_out.shape[0]
    tm = min(1024, m)
    tf = 256
    nf = d_ff // tf
    return pl.pallas_call(
        functools.partial(_ffn_kernel, final_norm=final_norm),
        grid=(m // tm, nf),
        in_specs=[
            pl.BlockSpec((tm, d), lambda i, f: (i, 0)),
            pl.BlockSpec((1, d), lambda i, f: (0, 0)),
            pl.BlockSpec((d, tf), lambda i, f: (0, f)),
            pl.BlockSpec((d, tf), lambda i, f: (0, f + nf)),
            pl.BlockSpec((tf, d), lambda i, f: (f, 0)),
            pl.BlockSpec((1, d), lambda i, f: (0, 0)),
        ],
        out_specs=pl.BlockSpec((tm, d), lambda i, f: (i, 0)),
        out_shape=jax.ShapeDtypeStruct((m, d), F32),
        scratch_shapes=[pltpu.VMEM((tm, d), BF16), pltpu.VMEM((tm, d), F32)],
        compiler_params=_params(("parallel", "arbitrary")),
        name="ffn",
    )(h, g.reshape(1, d), w_in, w_in, w_out, final_g.reshape(1, d))


def _proj_in_kernel(x_ref, g_ref, w_ref, o_ref, n_scr):
    @pl.when(pl.program_id(1) == 0)
    def _():
        n_scr[...] = _rms(x_ref[...], g_ref[...]).astype(BF16)

    o_ref[...] = _dot(n_scr[...], w_ref[...])


def _proj_in(h, g, w, tn):
    m, d = h.shape
    n_out = w.shape[1]
    tm = min(1024, m)
    return pl.pallas_call(
        _proj_in_kernel,
        grid=(m // tm, n_out // tn),
        in_specs=[
            pl.BlockSpec((tm, d), lambda i, j: (i, 0)),
            pl.BlockSpec((1, d), lambda i, j: (0, 0)),
            pl.BlockSpec((d, tn), lambda i, j: (0, j)),
        ],
        out_specs=pl.BlockSpec((tm, tn), lambda i, j: (i, j)),
        out_shape=jax.ShapeDtypeStruct((m, n_out), F32),
        scratch_shapes=[pltpu.VMEM((tm, d), BF16)],
        compiler_params=_params(("parallel", "arbitrary")),
        name="proj_in",
    )(h, g.reshape(1, d), w)


def _proj_out_kernel(h_ref, a_ref, b_ref, wa_ref, wb_ref, o_ref):
    o_ref[...] = h_ref[...] + _dot(a_ref[...], wa_ref[...]) + _dot(b_ref[...], wb_ref[...])


def _proj_out(h, a, b, w):
    m, d = h.shape
    ka, kb = a.shape[1], b.shape[1]
    tm = min(1024, m)
    return pl.pallas_call(
        _proj_out_kernel,
        grid=(m // tm,),
        in_specs=[
            pl.BlockSpec((tm, d), lambda i: (i, 0)),
            pl.BlockSpec((tm, ka), lambda i: (i, 0)),
            pl.BlockSpec((tm, kb), lambda i: (i, 0)),
            pl.BlockSpec((ka, d), lambda i: (0, 0)),
            pl.BlockSpec((kb, d), lambda i: (ka // kb, 0)),
        ],
        out_specs=pl.BlockSpec((tm, d), lambda i: (i, 0)),
        out_shape=jax.ShapeDtypeStruct((m, d), F32),
        compiler_params=_params(("parallel",)),
        name="proj_out",
    )(h, a, b, w, w)


def _swa_kernel(sink_ref, q_ref, kp_ref, kc_ref, vp_ref, vc_ref, o_ref):
    n = pl.program_id(1)
    qi = lax.broadcasted_iota(jnp.int32, (BLK, 2 * BLK), 0)
    kj = lax.broadcasted_iota(jnp.int32, (BLK, 2 * BLK), 1)
    dist = qi + BLK - kj
    valid = (dist >= 0) & (dist < WINDOW) & (n * BLK - BLK + kj >= 0)
    dist_f = dist.astype(F32)
    q = q_ref[0]
    k = jnp.concatenate([kp_ref[0], kc_ref[0]], axis=0).astype(BF16)
    v = jnp.concatenate([vp_ref[0], vc_ref[0]], axis=0).astype(BF16)
    scale = HEAD_DIM ** -0.5
    for kv in range(SWA_KV_HEADS):
        kh = k[:, kv * HEAD_DIM:(kv + 1) * HEAD_DIM]
        vh = v[:, kv * HEAD_DIM:(kv + 1) * HEAD_DIM]
        for g in range(SWA_GROUP):
            head = kv * SWA_GROUP + g
            slope = 2.0 ** (-8.0 * (head + 1) / SWA_HEADS)
            qh = q[:, head * HEAD_DIM:(head + 1) * HEAD_DIM].astype(BF16)
            s = _dot_nt(qh, kh) * scale - slope * dist_f
            s = jnp.where(valid, s, -jnp.inf)
            sink = sink_ref[head]
            m = jnp.maximum(jnp.max(s, axis=-1, keepdims=True), sink)
            p = jnp.exp(s - m)
            denom = jnp.sum(p, axis=-1, keepdims=True) + jnp.exp(sink - m)
            probs = (p / denom).astype(BF16)
            o_ref[0, :, head * HEAD_DIM:(head + 1) * HEAD_DIM] = _dot(probs, vh).astype(o_ref.dtype)


def _swa(proj, sinks):
    b, s, _ = proj.shape
    nb = s // BLK
    k_col = SWA_Q // SWA_KV
    v_col = k_col + 1
    prev = lambda n: jnp.maximum(n - 1, 0)
    return pl.pallas_call(
        _swa_kernel,
        grid=(b, nb),
        in_specs=[
            pl.BlockSpec(memory_space=pltpu.SMEM),
            pl.BlockSpec((1, BLK, SWA_Q), lambda i, n: (i, n, 0)),
            pl.BlockSpec((1, BLK, SWA_KV), lambda i, n: (i, prev(n), k_col)),
            pl.BlockSpec((1, BLK, SWA_KV), lambda i, n: (i, n, k_col)),
            pl.BlockSpec((1, BLK, SWA_KV), lambda i, n: (i, prev(n), v_col)),
            pl.BlockSpec((1, BLK, SWA_KV), lambda i, n: (i, n, v_col)),
        ],
        out_specs=pl.BlockSpec((1, BLK, SWA_Q), lambda i, n: (i, n, 0)),
        out_shape=jax.ShapeDtypeStruct((b, s, SWA_Q), BF16),
        compiler_params=_params(("parallel", "arbitrary")),
        name="swa",
    )(sinks, proj, proj, proj, proj, proj)


def _sb_kernel(q_ref, k_ref, v_ref, o_ref):
    nb = q_ref.shape[1] // BLK
    lane = lax.broadcasted_iota(jnp.int32, (BLK, BLK), 1)
    row = lax.broadcasted_iota(jnp.int32, (BLK, BLK), 0)
    strictly_causal = lane < row
    jj = lax.broadcasted_iota(jnp.int32, (BLK, 2 * BLK), 0)
    ss = lax.broadcasted_iota(jnp.int32, (BLK, 2 * BLK), 1)
    suffix_mat = jnp.where((jj > ss) | (ss >= BLK), 1.0, 0.0).astype(BF16)
    scale = HEAD_DIM ** -0.5

    def block_pair(qm, j, carry, acc, diagonal):
        rows = pl.ds(pl.multiple_of(j * BLK, BLK), BLK)
        kb = k_ref[0, rows, :].astype(BF16)
        vb = v_ref[0, rows, :].astype(BF16)
        z = _dot_nt(qm, kb)
        l1p = jnp.log1p(jnp.exp(-jnp.abs(z)))
        log_keep = -(jnp.maximum(z, 0.0) + l1p)
        log_beta = jnp.minimum(z, 0.0) - l1p
        if diagonal:
            log_keep = jnp.where(strictly_causal, log_keep, 0.0)
        cs = _split_dot(log_keep, suffix_mat)
        between = cs[:, :BLK] + carry
        carry = carry + cs[:, BLK:]
        w = jnp.exp(log_beta + between)
        if diagonal:
            w = jnp.where(strictly_causal, w, 0.0)
        acc = acc + _dot(w.astype(BF16), vb)
        return carry, acc

    def q_block(n, _):
        rows = pl.ds(pl.multiple_of(n * BLK, BLK), BLK)
        q = q_ref[0, rows, :] * scale
        outs = []
        for head in range(2):
            head_lanes = (lane >= head * HEAD_DIM) & (lane < (head + 1) * HEAD_DIM)
            qm = jnp.where(head_lanes, q, 0.0).astype(BF16)
            zero = jnp.zeros((BLK, BLK), F32)
            carry, acc = block_pair(qm, n, zero, zero, True)

            def kv_step(t, c):
                return block_pair(qm, n - 1 - t, c[0], c[1], False)

            carry, acc = lax.fori_loop(0, n, kv_step, (carry, acc))
            outs.append(acc)
        o_ref[0, rows, :] = jnp.where(lane < HEAD_DIM, outs[0], outs[1]).astype(o_ref.dtype)
        return 0

    lax.fori_loop(0, nb, q_block, 0)


def _sb(proj):
    b, s, _ = proj.shape
    q_col = (SWA_Q + 2 * SWA_KV) // BLK
    k_col = q_col + SB_W // BLK
    v_col = k_col + SB_W // BLK
    return pl.pallas_call(
        _sb_kernel,
        grid=(b, SB_W // BLK),
        in_specs=[
            pl.BlockSpec((1, s, BLK), lambda i, p: (i, 0, q_col + p)),
            pl.BlockSpec((1, s, BLK), lambda i, p: (i, 0, k_col + p)),
            pl.BlockSpec((1, s, BLK), lambda i, p: (i, 0, v_col + p)),
        ],
        out_specs=pl.BlockSpec((1, s, BLK), lambda i, p: (i, 0, p)),
        out_shape=jax.ShapeDtypeStruct((b, s, SB_W), BF16),
        compiler_params=_params(("parallel", "parallel")),
        name="stick_breaking",
    )(proj, proj, proj)


def _hgrn_kernel(layer, lbl_ref, ng_ref, q_ref, f_ref, i_ref, gate_ref, o_ref, state_scr):
    nc = q_ref.shape[1] // HG_CHUNK
    logits = lbl_ref[...]
    e = jnp.exp(logits - jnp.max(logits, axis=0, keepdims=True))
    sm = e / jnp.sum(e, axis=0, keepdims=True)
    lb = jnp.sum(sm[1:layer + 1], axis=0, keepdims=True)
    ti = lax.broadcasted_iota(jnp.int32, (HG_CHUNK, HG_CHUNK), 0)
    si = lax.broadcasted_iota(jnp.int32, (HG_CHUNK, HG_CHUNK), 1)
    cumsum_mat = jnp.where(si <= ti, 1.0, 0.0).astype(BF16)
    sub_row = lax.broadcasted_iota(jnp.int32, (HG_SUB, 1), 0)
    n_sub = HG_CHUNK // HG_SUB
    state_scr[...] = jnp.zeros_like(state_scr)

    def chunk(c, _):
        rows = pl.ds(pl.multiple_of(c * HG_CHUNK, HG_CHUNK), HG_CHUNK)
        z = f_ref[0, rows, :]
        qx = q_ref[0, rows, :]
        q = qx * jax.nn.sigmoid(qx)
        v = i_ref[0, rows, :]
        log_f = jnp.log(lb + (1.0 - lb) * jax.nn.sigmoid(z))
        k = (1.0 - lb) * jax.nn.sigmoid(-z)
        g = _split_dot_left(cumsum_mat, log_f)
        state = state_scr[...]
        o = _dot_nt((q * jnp.exp(g)).astype(BF16), state.astype(BF16))
        o_parts = [jnp.zeros((HG_SUB, HG_DK), F32) for _ in range(n_sub)]
        for j in range(n_sub - 1):
            lo, hi = j * HG_SUB, (j + 1) * HG_SUB
            r = g[hi - 1:hi, :]
            kt = (k[lo:hi] * jnp.exp(r - g[lo:hi])).astype(BF16)
            qt = (q[hi:] * jnp.exp(g[hi:] - r)).astype(BF16)
            sc = _dot_nt(qt, kt)
            contrib = _dot(sc.astype(BF16), v[lo:hi].astype(BF16))
            for i in range(j + 1, n_sub):
                o_parts[i] = o_parts[i] + contrib[(i - j - 1) * HG_SUB:(i - j) * HG_SUB]
        for i in range(n_sub):
            lo, hi = i * HG_SUB, (i + 1) * HG_SUB
            gi, qi, ki, vi = g[lo:hi], q[lo:hi], k[lo:hi], v[lo:hi]
            acc = o_parts[i]
            for s in range(HG_SUB):
                decay = jnp.exp(jnp.minimum(gi - gi[s:s + 1, :], 0.0))
                col = jnp.sum(qi * (ki[s:s + 1, :] * decay), axis=-1, keepdims=True)
                col = jnp.where(sub_row >= s, col, 0.0)
                acc = acc + col * vi[s:s + 1, :]
            o_parts[i] = acc
        o = o + jnp.concatenate(o_parts, axis=0)
        g_last = g[HG_CHUNK - 1:HG_CHUNK, :]
        kd = (k * jnp.exp(g_last - g)).astype(BF16)
        state_scr[...] = state * jnp.exp(g_last) + _dot_tn(v.astype(BF16), kd)
        gt = gate_ref[0, rows, :]
        o_ref[0, rows, :] = (_rms(o, ng_ref[...]) * (gt * jax.nn.sigmoid(gt))).astype(o_ref.dtype)
        return 0

    lax.fori_loop(0, nc, chunk, 0)


def _hgrn(proj, lb_logits, norm_g, layer):
    b, s, _ = proj.shape
    depth = lb_logits.shape[0]
    blk = lambda off: pl.BlockSpec((1, s, HG_DK), lambda i, h: (i, 0, off * HG_HEADS + h))
    return pl.pallas_call(
        functools.partial(_hgrn_kernel, layer),
        grid=(b, HG_HEADS),
        in_specs=[
            pl.BlockSpec((depth, HG_DK), lambda i, h: (0, h)),
            pl.BlockSpec((1, HG_DK), lambda i, h: (0, 0)),
            blk(0), blk(1), blk(2), blk(3),
        ],
        out_specs=pl.BlockSpec((1, s, HG_DK), lambda i, h: (i, 0, h)),
        out_shape=jax.ShapeDtypeStruct((b, s, HG_WIDTH), BF16),
        scratch_shapes=[pltpu.VMEM((HG_DK, HG_DK), F32)],
        compiler_params=_params(("parallel", "parallel")),
        name="hgrn2",
    )(lb_logits, norm_g.reshape(1, HG_DK), proj, proj, proj, proj)


CONV_HALO = 8


def _conv_kernel(w_ref, gb_ref, gc_ref, u_ref, gch_ref, uh_ref, o_ref):
    first = pl.program_id(1) == 0
    x = gc_ref[0] * u_ref[0]
    halo = jnp.where(first, 0.0, gch_ref[0] * uh_ref[0])
    row = lax.broadcasted_iota(jnp.int32, x.shape, 0)
    h1 = halo[CONV_HALO - 1:CONV_HALO, :]
    h2 = halo[CONV_HALO - 2:CONV_HALO - 1, :]
    x1 = jnp.where(row == 0, h1, pltpu.roll(x, 1, 0))
    x2 = jnp.where(row == 0, h2, jnp.where(row == 1, h1, pltpu.roll(x, 2, 0)))
    y = w_ref[0:1, :] * x2 + w_ref[1:2, :] * x1 + w_ref[2:3, :] * x
    o_ref[0] = (gb_ref[0] * y).astype(o_ref.dtype)


def _conv(proj, w):
    b, s, _ = proj.shape
    ts = min(512, s)
    c0 = 4 * HG_WIDTH // CONV_WIDTH
    main = lambda off: pl.BlockSpec((1, ts, CONV_WIDTH), lambda i, t: (i, t, c0 + off))
    halo = lambda off: pl.BlockSpec(
        (1, CONV_HALO, CONV_WIDTH),
        lambda i, t: (i, jnp.maximum(t * (ts // CONV_HALO) - 1, 0), c0 + off))
    return pl.pallas_call(
        _conv_kernel,
        grid=(b, s // ts),
        in_specs=[
            pl.BlockSpec((CONV_K, CONV_WIDTH), lambda i, t: (0, 0)),
            main(0), main(1), main(2), halo(1), halo(2),
        ],
        out_specs=pl.BlockSpec((1, ts, CONV_WIDTH), lambda i, t: (i, t, 0)),
        out_shape=jax.ShapeDtypeStruct((b, s, CONV_WIDTH), BF16),
        compiler_params=_params(("parallel", "arbitrary")),
        name="short_conv",
    )(w, proj, proj, proj, proj, proj)


def kernel(x, norm_g, ffn_w_in, ffn_w_out, attn_w_in, attn_sinks, attn_w_out, rec_w_in,
           hgrn_lb_logits, hgrn_norm_g, conv_w, rec_w_out, final_g):
    b, s, d = x.shape
    depth = norm_g.shape[0]
    m = b * s
    h = x.reshape(m, d)
    for layer in range(depth):
        h = _ffn(h, norm_g[layer, 0], ffn_w_in[layer, 0].astype(BF16), ffn_w_out[layer, 0].astype(BF16),
                 final_g, final_norm=False)
        if layer % 2 == 0:
            e = layer // 2
            proj = _proj_in(h, norm_g[layer, 1], attn_w_in[e].astype(BF16), 256).reshape(b, s, ATTN_IN)
            o_a = _swa(proj, attn_sinks[e])
            o_b = _sb(proj)
            h = _proj_out(h, o_a.reshape(m, SWA_Q), o_b.reshape(m, SB_W), attn_w_out[e].astype(BF16))
        else:
            o = layer // 2
            proj = _proj_in(h, norm_g[layer, 1], rec_w_in[o].astype(BF16), 512).reshape(b, s, REC_IN)
            o_c = _hgrn(proj, hgrn_lb_logits, hgrn_norm_g[o], layer)
            o_d = _conv(proj, conv_w[o])
            h = _proj_out(h, o_c.reshape(m, HG_WIDTH), o_d.reshape(m, CONV_WIDTH), rec_w_out[o].astype(BF16))
        h = _ffn(h, norm_g[layer, 2], ffn_w_in[layer, 1].astype(BF16), ffn_w_out[layer, 1].astype(BF16),
                 final_g, final_norm=(layer == depth - 1))
    return h.reshape(b, s, d)
```

```python
import functools
import math

import jax
import jax.numpy as jnp
from jax import lax
from jax.experimental import pallas as pl
from jax.experimental.pallas import tpu as pltpu

F32 = jnp.float32
BF16 = jnp.bfloat16

D_MODEL = 1024
HEAD_DIM = 64
SWA_HEADS = 8
SWA_KV_HEADS = 2
SWA_GROUP = SWA_HEADS // SWA_KV_HEADS
WINDOW = 128
BLK = 128
SB_HEADS = 8
HG_WIDTH = 512
HG_DK = 128
HG_HEADS = HG_WIDTH // HG_DK
HG_CHUNK = 64
HG_SUB = 16
CONV_WIDTH = 512
CONV_K = 3
D_FF = 2816
EPS = 1e-6

SWA_Q = SWA_HEADS * HEAD_DIM
SWA_KV = SWA_KV_HEADS * HEAD_DIM
SB_W = SB_HEADS * HEAD_DIM
ATTN_IN = SWA_Q + 2 * SWA_KV + 3 * SB_W
REC_IN = 4 * HG_WIDTH + 3 * CONV_WIDTH

VMEM_LIMIT_BYTES = 56 * 1024 * 1024


def _params(semantics):
    return pltpu.CompilerParams(dimension_semantics=semantics,
                                vmem_limit_bytes=VMEM_LIMIT_BYTES)


def _rms(x, g):
    return x * lax.rsqrt(jnp.mean(x * x, axis=-1, keepdims=True) + EPS) * g


def _dot(a, b):
    return jnp.dot(a, b, preferred_element_type=F32)


def _dot_nt(a, b):
    return lax.dot_general(a, b, (((1,), (1,)), ((), ())), preferred_element_type=F32)


def _dot_tn(a, b):
    return lax.dot_general(a, b, (((0,), (0,)), ((), ())), preferred_element_type=F32)


def _split_dot(x, m01):
    hi = x.astype(BF16)
    lo = (x - hi.astype(F32)).astype(BF16)
    return _dot(hi, m01) + _dot(lo, m01)


def _split_dot_left(m01, x):
    hi = x.astype(BF16)
    lo = (x - hi.astype(F32)).astype(BF16)
    return _dot(m01, hi) + _dot(m01, lo)


def _ffn_kernel(x_ref, g_ref, wg_ref, wu_ref, wo_ref, fg_ref, o_ref, n_scr, acc_scr, *, final_norm):
    f = pl.program_id(1)

    @pl.when(f == 0)
    def _():
        n_scr[...] = _rms(x_ref[...], g_ref[...]).astype(BF16)
        acc_scr[...] = jnp.zeros_like(acc_scr)

    n = n_scr[...]
    gate = _dot(n, wg_ref[...])
    up = _dot(n, wu_ref[...])
    act = (gate * jax.nn.sigmoid(gate) * up).astype(BF16)
    acc_scr[...] += _dot(act, wo_ref[...])

    @pl.when(f == pl.num_programs(1) - 1)
    def _():
        h = x_ref[...] + 0.5 * acc_scr[...]
        if final_norm:
            h = _rms(h, fg_ref[...])
        o_ref[...] = h


def _ffn(h, g, w_in, w_out, final_g, *, final_norm):
    m, d = h.shape
    d_ff = w_out.shape[0]
    tm = min(1024, m)
    tf = 256
    nf = d_ff // tf
    return pl.pallas_call(
        functools.partial(_ffn_kernel, final_norm=final_norm),
        grid=(m // tm, nf),
        in_specs=[
            pl.BlockSpec((tm, d), lambda i, f: (i, 0)),
            pl.BlockSpec((1, d), lambda i, f: (0, 0)),
            pl.BlockSpec((d, tf), lambda i, f: (0, f)),
            pl.BlockSpec((d, tf), lambda i, f: (0, f + nf)),
            pl.BlockSpec((tf, d), lambda i, f: (f, 0)),
            pl.BlockSpec((1, d), lambda i, f: (0, 0)),
        ],
        out_specs=pl.BlockSpec((tm, d), lambda i, f: (i, 0)),
        out_shape=jax.ShapeDtypeStruct((m, d), F32),
        scratch_shapes=[pltpu.VMEM((tm, d), BF16), pltpu.VMEM((tm, d), F32)],
        compiler_params=_params(("parallel", "arbitrary")),
        name="ffn",
    )(h, g.reshape(1, d), w_in, w_in, w_out, final_g.reshape(1, d))


def _proj_in_kernel(x_ref, g_ref, w_ref, o_ref, n_scr):
    @pl.when(pl.program_id(1) == 0)
    def _():
        n_scr[...] = _rms(x_ref[...], g_ref[...]).astype(BF16)

    o_ref[...] = _dot(n_scr[...], w_ref[...])


def _proj_in(h, g, w, tn):
    m, d = h.shape
    n_out = w.shape[1]
    tm = min(1024, m)
    return pl.pallas_call(
        _proj_in_kernel,
        grid=(m // tm, n_out // tn),
        in_specs=[
            pl.BlockSpec((tm, d), lambda i, j: (i, 0)),
            pl.BlockSpec((1, d), lambda i, j: (0, 0)),
            pl.BlockSpec((d, tn), lambda i, j: (0, j)),
        ],
        out_specs=pl.BlockSpec((tm, tn), lambda i, j: (i, j)),
        out_shape=jax.ShapeDtypeStruct((m, n_out), F32),
        scratch_shapes=[pltpu.VMEM((tm, d), BF16)],
        compiler_params=_params(("parallel", "arbitrary")),
        name="proj_in",
    )(h, g.reshape(1, d), w)


def _proj_out_kernel(h_ref, a_ref, b_ref, wa_ref, wb_ref, o_ref):
    o_ref[...] = h_ref[...] + _dot(a_ref[...], wa_ref[...]) + _dot(b_ref[...], wb_ref[...])


def _proj_out(h, a, b, w):
    m, d = h.shape
    ka, kb = a.shape[1], b.shape[1]
    tm = min(1024, m)
    return pl.pallas_call(
        _proj_out_kernel,
        grid=(m // tm,),
        in_specs=[
            pl.BlockSpec((tm, d), lambda i: (i, 0)),
            pl.BlockSpec((tm, ka), lambda i: (i, 0)),
            pl.BlockSpec((tm, kb), lambda i: (i, 0)),
            pl.BlockSpec((ka, d), lambda i: (0, 0)),
            pl.BlockSpec((kb, d), lambda i: (ka // kb, 0)),
        ],
        out_specs=pl.BlockSpec((tm, d), lambda i: (i, 0)),
        out_shape=jax.ShapeDtypeStruct((m, d), F32),
        compiler_params=_params(("parallel",)),
        name="proj_out",
    )(h, a, b, w, w)


def _swa_kernel(sink_ref, q_ref, kp_ref, kc_ref, vp_ref, vc_ref, o_ref):
    n = pl.program_id(1)
    qi = lax.broadcasted_iota(jnp.int32, (BLK, 2 * BLK), 0)
    kj = lax.broadcasted_iota(jnp.int32, (BLK, 2 * BLK), 1)
    dist = qi + BLK - kj
    valid = (dist >= 0) & (dist < WINDOW) & (n * BLK - BLK + kj >= 0)
    dist_f = dist.astype(F32)
    q = q_ref[0]
    k = jnp.concatenate([kp_ref[0], kc_ref[0]], axis=0).astype(BF16)
    v = jnp.concatenate([vp_ref[0], vc_ref[0]], axis=0).astype(BF16)
    scale = HEAD_DIM ** -0.5
    for kv in range(SWA_KV_HEADS):
        kh = k[:, kv * HEAD_DIM:(kv + 1) * HEAD_DIM]
        vh = v[:, kv * HEAD_DIM:(kv + 1) * HEAD_DIM]
        for g in range(SWA_GROUP):
            head = kv * SWA_GROUP + g
            slope = 2.0 ** (-8.0 * (head + 1) / SWA_HEADS)
            qh = q[:, head * HEAD_DIM:(head + 1) * HEAD_DIM].astype(BF16)
            s = _dot_nt(qh, kh) * scale - slope * dist_f
            s = jnp.where(valid, s, -jnp.inf)
            sink = sink_ref[head]
            m = jnp.maximum(jnp.max(s, axis=-1, keepdims=True), sink)
            p = jnp.exp(s - m)
            denom = jnp.sum(p, axis=-1, keepdims=True) + jnp.exp(sink - m)
            probs = (p / denom).astype(BF16)
            o_ref[0, :, head * HEAD_DIM:(head + 1) * HEAD_DIM] = _dot(probs, vh).astype(o_ref.dtype)


def _swa(proj, sinks):
    b, s, _ = proj.shape
    nb = s // BLK
    k_col = SWA_Q // SWA_KV
    v_col = k_col + 1
    prev = lambda n: jnp.maximum(n - 1, 0)
    return pl.pallas_call(
        _swa_kernel,
        grid=(b, nb),
        in_specs=[
            pl.BlockSpec(memory_space=pltpu.SMEM),
            pl.BlockSpec((1, BLK, SWA_Q), lambda i, n: (i, n, 0)),
            pl.BlockSpec((1, BLK, SWA_KV), lambda i, n: (i, prev(n), k_col)),
            pl.BlockSpec((1, BLK, SWA_KV), lambda i, n: (i, n, k_col)),
            pl.BlockSpec((1, BLK, SWA_KV), lambda i, n: (i, prev(n), v_col)),
            pl.BlockSpec((1, BLK, SWA_KV), lambda i, n: (i, n, v_col)),
        ],
        out_specs=pl.BlockSpec((1, BLK, SWA_Q), lambda i, n: (i, n, 0)),
        out_shape=jax.ShapeDtypeStruct((b, s, SWA_Q), BF16),
        compiler_params=_params(("parallel", "arbitrary")),
        name="swa",
    )(sinks, proj, proj, proj, proj, proj)


SB_TQ = 256
SB_KPI = 2


def _sb_kernel(q_ref, k_ref, v_ref, o_ref, kk_scr, vv_scr, carry_scr, acc_scr):
    s = q_ref.shape[1]
    nt = s // SB_TQ
    nb = s // BLK
    kpt = SB_TQ // BLK
    lane = lax.broadcasted_iota(jnp.int32, (BLK, BLK), 1)
    key_pos = lax.broadcasted_iota(jnp.int32, (SB_TQ, 2 * BLK), 1) & (BLK - 1)
    q_pos = lax.broadcasted_iota(jnp.int32, (SB_TQ, 2 * BLK), 0)
    jj = lax.broadcasted_iota(jnp.int32, (2 * BLK, 2 * BLK), 0) & (BLK - 1)
    ss = lax.broadcasted_iota(jnp.int32, (2 * BLK, 2 * BLK), 1)
    suffix_mat = jnp.where((jj > ss) | (ss >= BLK), 1.0, 0.0).astype(BF16)
    scale = HEAD_DIM ** -0.5

    def prep(j, _):
        rows = pl.ds(pl.multiple_of(j * BLK, BLK), BLK)
        kb = k_ref[0, rows, :]
        vb = v_ref[0, rows, :]
        kk_scr[j] = jnp.concatenate([jnp.where(lane < HEAD_DIM, kb, 0.0),
                                     jnp.where(lane >= HEAD_DIM, kb, 0.0)], axis=0).astype(BF16)
        vv_scr[j] = jnp.concatenate([jnp.where(lane < HEAD_DIM, vb, 0.0),
                                     jnp.where(lane >= HEAD_DIM, vb, 0.0)], axis=0).astype(BF16)
        return 0

    lax.fori_loop(0, nb, prep, 0)

    def block(qb, t, j, carry, masked):
        z = _dot_nt(qb, kk_scr[j])
        log_keep = -(jnp.maximum(z, 0.0) + jnp.log(1.0 + jnp.exp(-jnp.abs(z))))
        log_beta = z + log_keep
        if masked:
            causal = (j * BLK + key_pos) < (t * SB_TQ + q_pos)
            log_keep = jnp.where(causal, log_keep, 0.0)
        hi = log_keep.astype(BF16)
        lo = (log_keep - hi.astype(F32)).astype(BF16)
        cs = [_dot(jnp.concatenate([hi[:, h * BLK:(h + 1) * BLK], lo[:, h * BLK:(h + 1) * BLK]], axis=1),
                   suffix_mat) for h in range(2)]
        between = jnp.concatenate([cs[0][:, :BLK], cs[1][:, :BLK]], axis=1)
        total = jnp.concatenate([cs[0][:, BLK:], cs[1][:, BLK:]], axis=1)
        w = jnp.exp(log_beta + between + carry)
        if masked:
            w = jnp.where(causal, w, 0.0)
        return carry + total, _dot(w.astype(BF16), vv_scr[j])

    def q_tile(t, _):
        rows = pl.ds(pl.multiple_of(t * SB_TQ, SB_TQ), SB_TQ)
        qb = (q_ref[0, rows, :] * scale).astype(BF16)
        carry = jnp.zeros((SB_TQ, 2 * BLK), F32)
        acc = jnp.zeros((SB_TQ, BLK), F32)
        for d in range(kpt):
            carry, o = block(qb, t, kpt * t + kpt - 1 - d, carry, True)
            acc = acc + o
        carry_scr[...] = carry
        acc_scr[...] = acc

        def kv_step(i, _):
            carry = carry_scr[...]
            acc = acc_scr[...]
            for d in range(SB_KPI):
                carry, o = block(qb, t, kpt * t - 1 - i * SB_KPI - d, carry, False)
                acc = acc + o
            carry_scr[...] = carry
            acc_scr[...] = acc
            return 0

        lax.fori_loop(0, kpt * t // SB_KPI, kv_step, 0)
        o_ref[0, rows, :] = acc_scr[...].astype(o_ref.dtype)
        return 0

    lax.fori_loop(0, nt, q_tile, 0)


def _sb(proj):
    b, s, _ = proj.shape
    assert (SB_TQ // BLK) % SB_KPI == 0
    q_col = (SWA_Q + 2 * SWA_KV) // BLK
    k_col = q_col + SB_W // BLK
    v_col = k_col + SB_W // BLK
    return pl.pallas_call(
        _sb_kernel,
        grid=(b, SB_W // BLK),
        in_specs=[
            pl.BlockSpec((1, s, BLK), lambda i, p: (i, 0, q_col + p)),
            pl.BlockSpec((1, s, BLK), lambda i, p: (i, 0, k_col + p)),
            pl.BlockSpec((1, s, BLK), lambda i, p: (i, 0, v_col + p)),
        ],
        out_specs=pl.BlockSpec((1, s, BLK), lambda i, p: (i, 0, p)),
        out_shape=jax.ShapeDtypeStruct((b, s, SB_W), BF16),
        scratch_shapes=[pltpu.VMEM((s // BLK, 2 * BLK, BLK), BF16), pltpu.VMEM((s // BLK, 2 * BLK, BLK), BF16),
                        pltpu.VMEM((SB_TQ, 2 * BLK), F32), pltpu.VMEM((SB_TQ, BLK), F32)],
        compiler_params=_params(("parallel", "parallel")),
        name="stick_breaking",
    )(proj, proj, proj)


def _hgrn_kernel(layer, lbl_ref, ng_ref, q_ref, f_ref, i_ref, gate_ref, o_ref, state_scr):
    nc = q_ref.shape[1] // HG_CHUNK
    logits = lbl_ref[...]
    e = jnp.exp(logits - jnp.max(logits, axis=0, keepdims=True))
    sm = e / jnp.sum(e, axis=0, keepdims=True)
    lb = jnp.sum(sm[1:layer + 1], axis=0, keepdims=True)
    ti = lax.broadcasted_iota(jnp.int32, (HG_CHUNK, HG_CHUNK), 0)
    si = lax.broadcasted_iota(jnp.int32, (HG_CHUNK, HG_CHUNK), 1)
    cumsum_mat = jnp.where(si <= ti, 1.0, 0.0).astype(BF16)
    sub_row = lax.broadcasted_iota(jnp.int32, (HG_SUB, 1), 0)
    n_sub = HG_CHUNK // HG_SUB
    state_scr[...] = jnp.zeros_like(state_scr)

    def chunk(c, _):
        rows = pl.ds(pl.multiple_of(c * HG_CHUNK, HG_CHUNK), HG_CHUNK)
        z = f_ref[0, rows, :]
        qx = q_ref[0, rows, :]
        q = qx * jax.nn.sigmoid(qx)
        v = i_ref[0, rows, :]
        log_f = jnp.log(lb + (1.0 - lb) * jax.nn.sigmoid(z))
        k = (1.0 - lb) * jax.nn.sigmoid(-z)
        g = _split_dot_left(cumsum_mat, log_f)
        state = state_scr[...]
        o = _dot_nt((q * jnp.exp(g)).astype(BF16), state.astype(BF16))
        o_parts = [jnp.zeros((HG_SUB, HG_DK), F32) for _ in range(n_sub)]
        for j in range(n_sub - 1):
            lo, hi = j * HG_SUB, (j + 1) * HG_SUB
            r = g[hi - 1:hi, :]
            kt = (k[lo:hi] * jnp.exp(r - g[lo:hi])).astype(BF16)
            qt = (q[hi:] * jnp.exp(g[hi:] - r)).astype(BF16)
            sc = _dot_nt(qt, kt)
            contrib = _dot(sc.astype(BF16), v[lo:hi].astype(BF16))
            for i in range(j + 1, n_sub):
                o_parts[i] = o_parts[i] + contrib[(i - j - 1) * HG_SUB:(i - j) * HG_SUB]
        for i in range(n_sub):
            lo, hi = i * HG_SUB, (i + 1) * HG_SUB
            gi, qi, ki, vi = g[lo:hi], q[lo:hi], k[lo:hi], v[lo:hi]
            acc = o_parts[i]
            for s in range(HG_SUB):
                decay = jnp.exp(jnp.minimum(gi - gi[s:s + 1, :], 0.0))
                col = jnp.sum(qi * (ki[s:s + 1, :] * decay), axis=-1, keepdims=True)
                col = jnp.where(sub_row >= s, col, 0.0)
                acc = acc + col * vi[s:s + 1, :]
            o_parts[i] = acc
        o = o + jnp.concatenate(o_parts, axis=0)
        g_last = g[HG_CHUNK - 1:HG_CHUNK, :]
        kd = (k * jnp.exp(g_last - g)).astype(BF16)
        state_scr[...] = state * jnp.exp(g_last) + _dot_tn(v.astype(BF16), kd)
        gt = gate_ref[0, rows, :]
        o_ref[0, rows, :] = (_rms(o, ng_ref[...]) * (gt * jax.nn.sigmoid(gt))).astype(o_ref.dtype)
        return 0

    lax.fori_loop(0, nc, chunk, 0)


def _hgrn(proj, lb_logits, norm_g, layer):
    b, s, _ = proj.shape
    depth = lb_logits.shape[0]
    blk = lambda off: pl.BlockSpec((1, s, HG_DK), lambda i, h: (i, 0, off * HG_HEADS + h))
    return pl.pallas_call(
        functools.partial(_hgrn_kernel, layer),
        grid=(b, HG_HEADS),
        in_specs=[
            pl.BlockSpec((depth, HG_DK), lambda i, h: (0, h)),
            pl.BlockSpec((1, HG_DK), lambda i, h: (0, 0)),
            blk(0), blk(1), blk(2), blk(3),
        ],
        out_specs=pl.BlockSpec((1, s, HG_DK), lambda i, h: (i, 0, h)),
        out_shape=jax.ShapeDtypeStruct((b, s, HG_WIDTH), BF16),
        scratch_shapes=[pltpu.VMEM((HG_DK, HG_DK), F32)],
        compiler_params=_params(("parallel", "parallel")),
        name="hgrn2",
    )(lb_logits, norm_g.reshape(1, HG_DK), proj, proj, proj, proj)


CONV_HALO = 8


def _conv_kernel(w_ref, gb_ref, gc_ref, u_ref, gch_ref, uh_ref, o_ref):
    first = pl.program_id(1) == 0
    x = gc_ref[0] * u_ref[0]
    halo = jnp.where(first, 0.0, gch_ref[0] * uh_ref[0])
    row = lax.broadcasted_iota(jnp.int32, x.shape, 0)
    h1 = halo[CONV_HALO - 1:CONV_HALO, :]
    h2 = halo[CONV_HALO - 2:CONV_HALO - 1, :]
    x1 = jnp.where(row == 0, h1, pltpu.roll(x, 1, 0))
    x2 = jnp.where(row == 0, h2, jnp.where(row == 1, h1, pltpu.roll(x, 2, 0)))
    y = w_ref[0:1, :] * x2 + w_ref[1:2, :] * x1 + w_ref[2:3, :] * x
    o_ref[0] = (gb_ref[0] * y).astype(o_ref.dtype)


def _conv(proj, w):
    b, s, _ = proj.shape
    ts = min(512, s)
    c0 = 4 * HG_WIDTH // CONV_WIDTH
    main = lambda off: pl.BlockSpec((1, ts, CONV_WIDTH), lambda i, t: (i, t, c0 + off))
    halo = lambda off: pl.BlockSpec(
        (1, CONV_HALO, CONV_WIDTH),
        lambda i, t: (i, jnp.maximum(t * (ts // CONV_HALO) - 1, 0), c0 + off))
    return pl.pallas_call(
        _conv_kernel,
        grid=(b, s // ts),
        in_specs=[
            pl.BlockSpec((CONV_K, CONV_WIDTH), lambda i, t: (0, 0)),
            main(0), main(1), main(2), halo(1), halo(2),
        ],
        out_specs=pl.BlockSpec((1, ts, CONV_WIDTH), lambda i, t: (i, t, 0)),
        out_shape=jax.ShapeDtypeStruct((b, s, CONV_WIDTH), BF16),
        compiler_params=_params(("parallel", "arbitrary")),
        name="short_conv",
    )(w, proj, proj, proj, proj, proj)


def kernel(x, norm_g, ffn_w_in, ffn_w_out, attn_w_in, attn_sinks, attn_w_out, rec_w_in,
           hgrn_lb_logits, hgrn_norm_g, conv_w, rec_w_out, final_g):
    b, s, d = x.shape
    depth = norm_g.shape[0]
    m = b * s
    h = x.reshape(m, d)
    for layer in range(depth):
        h = _ffn(h, norm_g[layer, 0], ffn_w_in[layer, 0].astype(BF16), ffn_w_out[layer, 0].astype(BF16),
                 final_g, final_norm=False)
        if layer % 2 == 0:
            e = layer // 2
            proj = _proj_in(h, norm_g[layer, 1], attn_w_in[e].astype(BF16), 256).reshape(b, s, ATTN_IN)
            o_a = _swa(proj, attn_sinks[e])
            o_b = _sb(proj)
            h = _proj_out(h, o_a.reshape(m, SWA_Q), o_b.reshape(m, SB_W), attn_w_out[e].astype(BF16))
        else:
            o = layer // 2
            proj = _proj_in(h, norm_g[layer, 1], rec_w_in[o].astype(BF16), 512).reshape(b, s, REC_IN)
            o_c = _hgrn(proj, hgrn_lb_logits, hgrn_norm_g[o], layer)
            o_d = _conv(proj, conv_w[o])
            h = _proj_out(h, o_c.reshape(m, HG_WIDTH), o_d.reshape(m, CONV_WIDTH), rec_w_out[o].astype(BF16))
        h = _ffn(h, norm_g[layer, 2], ffn_w_in[layer, 1].astype(BF16), ffn_w_out[layer, 1].astype(BF16),
                 final_g, final_norm=(layer == depth - 1))
    return h.reshape(b, s, d)
```

```python
import functools
import math

import jax
import jax.numpy as jnp
from jax import lax
from jax.experimental import pallas as pl
from jax.experimental.pallas import tpu as pltpu

F32 = jnp.float32
BF16 = jnp.bfloat16

D_MODEL = 1024
HEAD_DIM = 64
SWA_HEADS = 8
SWA_KV_HEADS = 2
SWA_GROUP = SWA_HEADS // SWA_KV_HEADS
WINDOW = 128
BLK = 128
SUBLANES = 8
SB_HEADS = 8
HG_WIDTH = 512
HG_DK = 128
HG_HEADS = HG_WIDTH // HG_DK
HG_CHUNK = 64
HG_SUB = 16
CONV_WIDTH = 512
CONV_K = 3
D_FF = 2816
EPS = 1e-6

SWA_Q = SWA_HEADS * HEAD_DIM
SWA_KV = SWA_KV_HEADS * HEAD_DIM
SB_W = SB_HEADS * HEAD_DIM
ATTN_IN = SWA_Q + 2 * SWA_KV + 3 * SB_W
REC_IN = 4 * HG_WIDTH + 3 * CONV_WIDTH

VMEM_LIMIT_BYTES = 56 * 1024 * 1024


def _params(semantics):
    return pltpu.CompilerParams(dimension_semantics=semantics,
                                vmem_limit_bytes=VMEM_LIMIT_BYTES)


def _rms(x, g):
    return x * lax.rsqrt(jnp.mean(x * x, axis=-1, keepdims=True) + EPS) * g


def _dot(a, b):
    return jnp.dot(a, b, preferred_element_type=F32)


def _dot_nt(a, b):
    return lax.dot_general(a, b, (((1,), (1,)), ((), ())), preferred_element_type=F32)


def _dot_tn(a, b):
    return lax.dot_general(a, b, (((0,), (0,)), ((), ())), preferred_element_type=F32)


def _split_dot(x, m01):
    hi = x.astype(BF16)
    lo = (x - hi.astype(F32)).astype(BF16)
    return _dot(hi, m01) + _dot(lo, m01)


def _split_dot_left(m01, x):
    hi = x.astype(BF16)
    lo = (x - hi.astype(F32)).astype(BF16)
    return _dot(m01, hi) + _dot(m01, lo)


def _ffn_kernel(x_ref, g_ref, wi_ref, wo_ref, fg_ref, o_ref, *, final_norm):
    d_ff = wo_ref.shape[0]
    x = x_ref[...]
    n = _rms(x, g_ref[...]).astype(BF16)
    gate = _dot(n, wi_ref[:, :d_ff])
    up = _dot(n, wi_ref[:, d_ff:])
    act = (gate * jax.nn.sigmoid(gate) * up).astype(BF16)
    h = x + 0.5 * _dot(act, wo_ref[...])
    if final_norm:
        h = _rms(h, fg_ref[...])
    o_ref[...] = h


FFN_TM = 512


def _ffn(h, g, w_in, w_out, final_g, *, final_norm):
    m, d = h.shape
    d_ff = w_out.shape[0]
    tm = min(FFN_TM, m)
    resident = lambda shape: pl.BlockSpec(shape, lambda i: (0, 0), pipeline_mode=pl.Buffered(1))
    return pl.pallas_call(
        functools.partial(_ffn_kernel, final_norm=final_norm),
        grid=(m // tm,),
        in_specs=[
            pl.BlockSpec((tm, d), lambda i: (i, 0)),
            resident((1, d)),
            resident((d, 2 * d_ff)),
            resident((d_ff, d)),
            resident((1, d)),
        ],
        out_specs=pl.BlockSpec((tm, d), lambda i: (i, 0)),
        out_shape=jax.ShapeDtypeStruct((m, d), F32),
        compiler_params=_params(("parallel",)),
        name="ffn",
    )(h, g.reshape(1, d), w_in, w_out, final_g.reshape(1, d))


def _proj_in_kernel(x_ref, g_ref, w_ref, o_ref, *wide_refs, wide_cols):
    y = _dot(_rms(x_ref[...], g_ref[...]).astype(BF16), w_ref[...])
    o_ref[...] = y.astype(o_ref.dtype)
    if wide_cols is not None:
        wide_refs[0][...] = y[:, wide_cols[0]:wide_cols[1]]


PROJ_TM = 512


def _proj_in(h, g, w, wide_cols=None):
    m, d = h.shape
    n_out = w.shape[1]
    tm = min(PROJ_TM, m)
    resident = lambda shape: pl.BlockSpec(shape, lambda i: (0, 0), pipeline_mode=pl.Buffered(1))
    out_specs = [pl.BlockSpec((tm, n_out), lambda i: (i, 0))]
    out_shape = [jax.ShapeDtypeStruct((m, n_out), BF16)]
    if wide_cols is not None:
        out_specs.append(pl.BlockSpec((tm, wide_cols[1] - wide_cols[0]), lambda i: (i, 0)))
        out_shape.append(jax.ShapeDtypeStruct((m, wide_cols[1] - wide_cols[0]), F32))
    return pl.pallas_call(
        functools.partial(_proj_in_kernel, wide_cols=wide_cols),
        grid=(m // tm,),
        in_specs=[
            pl.BlockSpec((tm, d), lambda i: (i, 0)),
            resident((1, d)),
            resident((d, n_out)),
        ],
        out_specs=out_specs,
        out_shape=out_shape,
        compiler_params=_params(("parallel",)),
        name="proj_in",
    )(h, g.reshape(1, d), w)


def _proj_out_kernel(h_ref, a_ref, b_ref, wa_ref, wb_ref, o_ref):
    o_ref[...] = h_ref[...] + _dot(a_ref[...], wa_ref[...]) + _dot(b_ref[...], wb_ref[...])


def _proj_out(h, a, b, w):
    m, d = h.shape
    ka, kb = a.shape[1], b.shape[1]
    tm = min(1024, m)
    return pl.pallas_call(
        _proj_out_kernel,
        grid=(m // tm,),
        in_specs=[
            pl.BlockSpec((tm, d), lambda i: (i, 0)),
            pl.BlockSpec((tm, ka), lambda i: (i, 0)),
            pl.BlockSpec((tm, kb), lambda i: (i, 0)),
            pl.BlockSpec((ka, d), lambda i: (0, 0)),
            pl.BlockSpec((kb, d), lambda i: (ka // kb, 0)),
        ],
        out_specs=pl.BlockSpec((tm, d), lambda i: (i, 0)),
        out_shape=jax.ShapeDtypeStruct((m, d), F32),
        compiler_params=_params(("parallel",)),
        name="proj_out",
    )(h, a, b, w, w)


def _swa_kernel(sink_ref, q_ref, kp_ref, kc_ref, vp_ref, vc_ref, o_ref):
    n = pl.program_id(1)
    qi = lax.broadcasted_iota(jnp.int32, (BLK, 2 * BLK), 0)
    kj = lax.broadcasted_iota(jnp.int32, (BLK, 2 * BLK), 1)
    dist = qi + BLK - kj
    valid = (dist >= 0) & (dist < WINDOW) & (n * BLK - BLK + kj >= 0)
    dist_f = dist.astype(F32)
    q = q_ref[0]
    k = jnp.concatenate([kp_ref[0], kc_ref[0]], axis=0).astype(BF16)
    v = jnp.concatenate([vp_ref[0], vc_ref[0]], axis=0).astype(BF16)
    scale = HEAD_DIM ** -0.5
    for kv in range(SWA_KV_HEADS):
        kh = k[:, kv * HEAD_DIM:(kv + 1) * HEAD_DIM]
        vh = v[:, kv * HEAD_DIM:(kv + 1) * HEAD_DIM]
        for g in range(SWA_GROUP):
            head = kv * SWA_GROUP + g
            slope = 2.0 ** (-8.0 * (head + 1) / SWA_HEADS)
            qh = q[:, head * HEAD_DIM:(head + 1) * HEAD_DIM].astype(BF16)
            s = _dot_nt(qh, kh) * scale - slope * dist_f
            s = jnp.where(valid, s, -jnp.inf)
            sink = sink_ref[head]
            m = jnp.maximum(jnp.max(s, axis=-1, keepdims=True), sink)
            p = jnp.exp(s - m)
            denom = jnp.sum(p, axis=-1, keepdims=True) + jnp.exp(sink - m)
            probs = (p / denom).astype(BF16)
            o_ref[0, :, head * HEAD_DIM:(head + 1) * HEAD_DIM] = _dot(probs, vh).astype(o_ref.dtype)


def _swa(proj, sinks):
    b, s, _ = proj.shape
    nb = s // BLK
    q_col = 3 * SB_W // SWA_Q
    k_col = (3 * SB_W + SWA_Q) // SWA_KV
    v_col = k_col + 1
    prev = lambda n: jnp.maximum(n - 1, 0)
    return pl.pallas_call(
        _swa_kernel,
        grid=(b, nb),
        in_specs=[
            pl.BlockSpec(memory_space=pltpu.SMEM),
            pl.BlockSpec((1, BLK, SWA_Q), lambda i, n: (i, n, q_col)),
            pl.BlockSpec((1, BLK, SWA_KV), lambda i, n: (i, prev(n), k_col)),
            pl.BlockSpec((1, BLK, SWA_KV), lambda i, n: (i, n, k_col)),
            pl.BlockSpec((1, BLK, SWA_KV), lambda i, n: (i, prev(n), v_col)),
            pl.BlockSpec((1, BLK, SWA_KV), lambda i, n: (i, n, v_col)),
        ],
        out_specs=pl.BlockSpec((1, BLK, SWA_Q), lambda i, n: (i, n, 0)),
        out_shape=jax.ShapeDtypeStruct((b, s, SWA_Q), BF16),
        compiler_params=_params(("parallel", "arbitrary")),
        name="swa",
    )(sinks, proj, proj, proj, proj, proj)


SB_TQ = 256
SB_KPI = 2
LOG2E = math.log2(math.e)


SB_PAIRS = 4


def _sb_kernel(q_ref, k_ref, v_ref, o_ref, kk_scr, vv_scr, carry_scr, acc_scr):
    s = q_ref.shape[1]
    nt = s // SB_TQ
    nb = s // BLK
    kpt = SB_TQ // BLK
    lane = lax.broadcasted_iota(jnp.int32, (BLK, BLK), 1)
    key_pos = lax.broadcasted_iota(jnp.int32, (SB_TQ, 2 * BLK), 1) & (BLK - 1)
    q_pos = lax.broadcasted_iota(jnp.int32, (SB_TQ, 2 * BLK), 0)
    jj = lax.broadcasted_iota(jnp.int32, (2 * BLK, 2 * BLK), 0) & (BLK - 1)
    ss = lax.broadcasted_iota(jnp.int32, (2 * BLK, 2 * BLK), 1)
    suffix_mat = jnp.where((jj > ss) | (ss >= BLK), 1.0, 0.0).astype(BF16)
    scale = HEAD_DIM ** -0.5

    def prep(j, _):
        rows = pl.ds(pl.multiple_of(j * BLK, BLK), BLK)
        for p in range(SB_PAIRS):
            kb = k_ref[0, rows, p * BLK:(p + 1) * BLK]
            vb = v_ref[0, rows, p * BLK:(p + 1) * BLK]
            kk_scr[p, j] = jnp.concatenate([jnp.where(lane < HEAD_DIM, kb, 0.0),
                                            jnp.where(lane >= HEAD_DIM, kb, 0.0)], axis=0).astype(BF16)
            vv_scr[p, j] = jnp.concatenate([jnp.where(lane < HEAD_DIM, vb, 0.0),
                                            jnp.where(lane >= HEAD_DIM, vb, 0.0)], axis=0).astype(BF16)
        return 0

    lax.fori_loop(0, nb, prep, 0)

    def block(qn, p, t, j, carry, masked):
        nz = _dot_nt(qn, kk_scr[p, j])
        log_keep = jnp.minimum(nz, 0.0) - jnp.log2(1.0 + jnp.exp2(-jnp.abs(nz)))
        log_beta = log_keep - nz
        if masked:
            causal = (j * BLK + key_pos) < (t * SB_TQ + q_pos)
            log_keep = jnp.where(causal, log_keep, 0.0)
        hi = log_keep.astype(BF16)
        lo = (log_keep - hi.astype(F32)).astype(BF16)
        cs = [_dot(jnp.concatenate([hi[:, h * BLK:(h + 1) * BLK], lo[:, h * BLK:(h + 1) * BLK]], axis=1),
                   suffix_mat) for h in range(2)]
        between = jnp.concatenate([cs[0][:, :BLK], cs[1][:, :BLK]], axis=1)
        total = jnp.concatenate([cs[0][:, BLK:], cs[1][:, BLK:]], axis=1)
        w = jnp.exp2(log_beta + between + carry)
        if masked:
            w = jnp.where(causal, w, 0.0)
        return carry + total, _dot(w.astype(BF16), vv_scr[p, j])

    def q_tile(t, _):
        rows = pl.ds(pl.multiple_of(t * SB_TQ, SB_TQ), SB_TQ)
        qn = [(q_ref[0, rows, p * BLK:(p + 1) * BLK].astype(F32) * (-scale * LOG2E)).astype(BF16)
              for p in range(SB_PAIRS)]
        for p in range(SB_PAIRS):
            carry = jnp.zeros((SB_TQ, 2 * BLK), F32)
            acc = jnp.zeros((SB_TQ, BLK), F32)
            for d in range(kpt):
                carry, o = block(qn[p], p, t, kpt * t + kpt - 1 - d, carry, True)
                acc = acc + o
            carry_scr[p] = carry
            acc_scr[p] = acc

        def kv_step(i, _):
            for p in range(SB_PAIRS):
                carry = carry_scr[p]
                acc = acc_scr[p]
                for d in range(SB_KPI):
                    carry, o = block(qn[p], p, t, kpt * t - 1 - i * SB_KPI - d, carry, False)
                    acc = acc + o
                carry_scr[p] = carry
                acc_scr[p] = acc
            return 0

        lax.fori_loop(0, kpt * t // SB_KPI, kv_step, 0)
        for p in range(SB_PAIRS):
            o_ref[0, rows, p * BLK:(p + 1) * BLK] = acc_scr[p].astype(o_ref.dtype)
        return 0

    lax.fori_loop(0, nt, q_tile, 0)


def _sb(proj):
    b, s, _ = proj.shape
    assert (SB_TQ // BLK) % SB_KPI == 0
    width = SB_PAIRS * BLK
    q_col = 0
    k_col = SB_W // width
    v_col = 2 * SB_W // width
    return pl.pallas_call(
        _sb_kernel,
        grid=(b, SB_W // width),
        in_specs=[
            pl.BlockSpec((1, s, width), lambda i, p: (i, 0, q_col + p)),
            pl.BlockSpec((1, s, width), lambda i, p: (i, 0, k_col + p)),
            pl.BlockSpec((1, s, width), lambda i, p: (i, 0, v_col + p)),
        ],
        out_specs=pl.BlockSpec((1, s, width), lambda i, p: (i, 0, p)),
        out_shape=jax.ShapeDtypeStruct((b, s, SB_W), BF16),
        scratch_shapes=[pltpu.VMEM((SB_PAIRS, s // BLK, 2 * BLK, BLK), BF16),
                        pltpu.VMEM((SB_PAIRS, s // BLK, 2 * BLK, BLK), BF16),
                        pltpu.VMEM((SB_PAIRS, SB_TQ, 2 * BLK), F32), pltpu.VMEM((SB_PAIRS, SB_TQ, BLK), F32)],
        compiler_params=_params(("parallel", "parallel")),
        name="stick_breaking",
    )(proj, proj, proj)


def _hgrn_kernel(layer, lbl_ref, ng_ref, q_ref, f_ref, i_ref, gate_ref, o_ref, state_scr):
    nc = q_ref.shape[1] // HG_CHUNK
    logits = lbl_ref[...]
    e = jnp.exp(logits - jnp.max(logits, axis=0, keepdims=True))
    sm = e / jnp.sum(e, axis=0, keepdims=True)
    lb = jnp.sum(sm[1:layer + 1], axis=0, keepdims=True)
    ti = lax.broadcasted_iota(jnp.int32, (HG_CHUNK, HG_CHUNK), 0)
    si = lax.broadcasted_iota(jnp.int32, (HG_CHUNK, HG_CHUNK), 1)
    cumsum_mat = jnp.where(si <= ti, 1.0, 0.0).astype(BF16)
    row8 = lax.broadcasted_iota(jnp.int32, (SUBLANES, 1), 0)
    n_sub = HG_CHUNK // HG_SUB
    state_scr[...] = jnp.zeros_like(state_scr)

    def pair_rows(acc, qt, gt, ks, gs, vs, s, masked):
        decay = jnp.exp2(gt - gs[s:s + 1, :])
        col = jnp.sum(qt * (ks[s:s + 1, :] * decay), axis=-1, keepdims=True)
        if masked:
            col = jnp.where(row8 >= s, col, 0.0)
        return acc + col * vs[s:s + 1, :]

    def chunk(c, _):
        rows = pl.ds(pl.multiple_of(c * HG_CHUNK, HG_CHUNK), HG_CHUNK)
        z = f_ref[0, rows, :]
        qx = q_ref[0, rows, :].astype(F32)
        v_all = i_ref[0, rows, :].astype(F32)
        gate = gate_ref[0, rows, :].astype(F32)
        sig = jax.nn.sigmoid(z)
        q_all = qx * jax.nn.sigmoid(qx)
        k_all = (1.0 - lb) * (1.0 - sig)
        log2_f = jnp.log2(lb + (1.0 - lb) * sig)
        g_all = _split_dot_left(cumsum_mat, log2_f)
        g_last_all = g_all[HG_CHUNK - 1:HG_CHUNK, :]
        qg_all = (q_all * jnp.exp2(g_all)).astype(BF16)
        kd_all = (k_all * jnp.exp2(g_last_all - g_all)).astype(BF16)
        gated = gate * jax.nn.sigmoid(gate)
        kt_all, qt_all = [], []
        for j in range(n_sub - 1):
            lo, hi = j * HG_SUB, (j + 1) * HG_SUB
            r = g_all[hi - 1:hi, :]
            kt_all.append((k_all[lo:hi] * jnp.exp2(r - g_all[lo:hi])).astype(BF16))
            qt_all.append((q_all[hi:] * jnp.exp2(g_all[hi:] - r)).astype(BF16))
        for h in range(HG_HEADS):
            cols = slice(h * HG_DK, (h + 1) * HG_DK)
            g, q, k, v = g_all[:, cols], q_all[:, cols], k_all[:, cols], v_all[:, cols]
            v_bf = v.astype(BF16)
            state = state_scr[h]
            o = _dot_nt(qg_all[:, cols], state.astype(BF16))
            o_parts = [jnp.zeros((HG_SUB, HG_DK), F32) for _ in range(n_sub)]
            for j in range(n_sub - 1):
                lo, hi = j * HG_SUB, (j + 1) * HG_SUB
                sc = _dot_nt(qt_all[j][:, cols], kt_all[j][:, cols])
                contrib = _dot(sc.astype(BF16), v_bf[lo:hi])
                for i in range(j + 1, n_sub):
                    o_parts[i] = o_parts[i] + contrib[(i - j - 1) * HG_SUB:(i - j) * HG_SUB]
            tiles = []
            for i in range(n_sub):
                lo, mid, hi = i * HG_SUB, i * HG_SUB + SUBLANES, (i + 1) * HG_SUB
                acc_a, acc_b = o_parts[i][:SUBLANES], o_parts[i][SUBLANES:]
                for s in range(SUBLANES):
                    acc_a = pair_rows(acc_a, q[lo:mid], g[lo:mid], k[lo:mid], g[lo:mid], v[lo:mid], s, True)
                    acc_b = pair_rows(acc_b, q[mid:hi], g[mid:hi], k[lo:mid], g[lo:mid], v[lo:mid], s, False)
                    acc_b = pair_rows(acc_b, q[mid:hi], g[mid:hi], k[mid:hi], g[mid:hi], v[mid:hi], s, True)
                tiles += [acc_a, acc_b]
            o = o + jnp.concatenate(tiles, axis=0)
            state_scr[h] = state * jnp.exp2(g_last_all[:, cols]) + _dot_tn(v_bf, kd_all[:, cols])
            o_ref[0, rows, cols] = (_rms(o, ng_ref[...]) * gated[:, cols]).astype(o_ref.dtype)
        return 0

    lax.fori_loop(0, nc, chunk, 0)


def _hgrn(proj, f_logit, lb_logits, norm_g, layer):
    b, s, _ = proj.shape
    depth = lb_logits.shape[0]
    assert HG_SUB == 2 * SUBLANES
    blk = lambda off: pl.BlockSpec((1, s, HG_WIDTH), lambda i: (i, 0, off))
    return pl.pallas_call(
        functools.partial(_hgrn_kernel, layer),
        grid=(b,),
        in_specs=[
            pl.BlockSpec((depth, HG_WIDTH), lambda i: (0, 0)),
            pl.BlockSpec((1, HG_DK), lambda i: (0, 0)),
            blk(0), blk(0), blk(2), blk(3),
        ],
        out_specs=pl.BlockSpec((1, s, HG_WIDTH), lambda i: (i, 0, 0)),
        out_shape=jax.ShapeDtypeStruct((b, s, HG_WIDTH), BF16),
        scratch_shapes=[pltpu.VMEM((HG_HEADS, HG_DK, HG_DK), F32)],
        compiler_params=_params(("parallel",)),
        name="hgrn2",
    )(lb_logits, norm_g.reshape(1, HG_DK), proj, f_logit, proj, proj)


CONV_HALO = 16


def _conv_kernel(w_ref, gb_ref, gc_ref, u_ref, gch_ref, uh_ref, o_ref):
    first = pl.program_id(1) == 0
    x = gc_ref[0].astype(F32) * u_ref[0].astype(F32)
    halo = jnp.where(first, 0.0, gch_ref[0].astype(F32) * uh_ref[0].astype(F32))
    row = lax.broadcasted_iota(jnp.int32, x.shape, 0)
    h1 = halo[CONV_HALO - 1:CONV_HALO, :]
    h2 = halo[CONV_HALO - 2:CONV_HALO - 1, :]
    x1 = jnp.where(row == 0, h1, pltpu.roll(x, 1, 0))
    x2 = jnp.where(row == 0, h2, jnp.where(row == 1, h1, pltpu.roll(x, 2, 0)))
    y = w_ref[0:1, :] * x2 + w_ref[1:2, :] * x1 + w_ref[2:3, :] * x
    o_ref[0] = (gb_ref[0].astype(F32) * y).astype(o_ref.dtype)


def _conv(proj, w):
    b, s, _ = proj.shape
    ts = min(512, s)
    c0 = 4 * HG_WIDTH // CONV_WIDTH
    main = lambda off: pl.BlockSpec((1, ts, CONV_WIDTH), lambda i, t: (i, t, c0 + off))
    halo = lambda off: pl.BlockSpec(
        (1, CONV_HALO, CONV_WIDTH),
        lambda i, t: (i, jnp.maximum(t * (ts // CONV_HALO) - 1, 0), c0 + off))
    return pl.pallas_call(
        _conv_kernel,
        grid=(b, s // ts),
        in_specs=[
            pl.BlockSpec((CONV_K, CONV_WIDTH), lambda i, t: (0, 0)),
            main(0), main(1), main(2), halo(1), halo(2),
        ],
        out_specs=pl.BlockSpec((1, ts, CONV_WIDTH), lambda i, t: (i, t, 0)),
        out_shape=jax.ShapeDtypeStruct((b, s, CONV_WIDTH), BF16),
        compiler_params=_params(("parallel", "arbitrary")),
        name="short_conv",
    )(w, proj, proj, proj, proj, proj)


def _attn_columns(w_in):
    split = SWA_Q + 2 * SWA_KV
    return jnp.concatenate([w_in[:, split:], w_in[:, :split]], axis=1)


def kernel(x, norm_g, ffn_w_in, ffn_w_out, attn_w_in, attn_sinks, attn_w_out, rec_w_in,
           hgrn_lb_logits, hgrn_norm_g, conv_w, rec_w_out, final_g):
    b, s, d = x.shape
    depth = norm_g.shape[0]
    m = b * s
    h = x.reshape(m, d)
    for layer in range(depth):
        h = _ffn(h, norm_g[layer, 0], ffn_w_in[layer, 0].astype(BF16), ffn_w_out[layer, 0].astype(BF16),
                 final_g, final_norm=False)
        if layer % 2 == 0:
            e = layer // 2
            proj = _proj_in(h, norm_g[layer, 1], _attn_columns(attn_w_in[e]).astype(BF16))[0]
            proj = proj.reshape(b, s, ATTN_IN)
            o_a = _swa(proj, attn_sinks[e])
            o_b = _sb(proj)
            h = _proj_out(h, o_a.reshape(m, SWA_Q), o_b.reshape(m, SB_W), attn_w_out[e].astype(BF16))
        else:
            o = layer // 2
            proj, f_logit = _proj_in(h, norm_g[layer, 1], rec_w_in[o].astype(BF16), (HG_WIDTH, 2 * HG_WIDTH))
            proj = proj.reshape(b, s, REC_IN)
            o_c = _hgrn(proj, f_logit.reshape(b, s, HG_WIDTH), hgrn_lb_logits, hgrn_norm_g[o], layer)
            o_d = _conv(proj, conv_w[o])
            h = _proj_out(h, o_c.reshape(m, HG_WIDTH), o_d.reshape(m, CONV_WIDTH), rec_w_out[o].astype(BF16))
        h = _ffn(h, norm_g[layer, 2], ffn_w_in[layer, 1].astype(BF16), ffn_w_out[layer, 1].astype(BF16),
                 final_g, final_norm=(layer == depth - 1))
    return h.reshape(b, s, d)
```

```python
import functools
import math

import jax
import jax.numpy as jnp
from jax import lax
from jax.experimental import pallas as pl
from jax.experimental.pallas import tpu as pltpu

F32 = jnp.float32
BF16 = jnp.bfloat16

D_MODEL = 1024
HEAD_DIM = 64
SWA_HEADS = 8
SWA_KV_HEADS = 2
SWA_GROUP = SWA_HEADS // SWA_KV_HEADS
WINDOW = 128
BLK = 128
SUBLANES = 8
SB_HEADS = 8
HG_WIDTH = 512
HG_DK = 128
HG_HEADS = HG_WIDTH // HG_DK
HG_CHUNK = 64
HG_SUB = 16
CONV_WIDTH = 512
CONV_K = 3
D_FF = 2816
EPS = 1e-6

SWA_Q = SWA_HEADS * HEAD_DIM
SWA_KV = SWA_KV_HEADS * HEAD_DIM
SB_W = SB_HEADS * HEAD_DIM
ATTN_IN = SWA_Q + 2 * SWA_KV + 3 * SB_W
REC_IN = 4 * HG_WIDTH + 3 * CONV_WIDTH

VMEM_LIMIT_BYTES = 56 * 1024 * 1024


def _params(semantics):
    return pltpu.CompilerParams(dimension_semantics=semantics,
                                vmem_limit_bytes=VMEM_LIMIT_BYTES)


def _rms(x, g):
    return x * lax.rsqrt(jnp.mean(x * x, axis=-1, keepdims=True) + EPS) * g


def _dot(a, b):
    return jnp.dot(a, b, preferred_element_type=F32)


def _dot_nt(a, b):
    return lax.dot_general(a, b, (((1,), (1,)), ((), ())), preferred_element_type=F32)


def _dot_tn(a, b):
    return lax.dot_general(a, b, (((0,), (0,)), ((), ())), preferred_element_type=F32)


def _split_dot(x, m01):
    hi = x.astype(BF16)
    lo = (x - hi.astype(F32)).astype(BF16)
    return _dot(hi, m01) + _dot(lo, m01)


def _split_dot_left(m01, x):
    hi = x.astype(BF16)
    lo = (x - hi.astype(F32)).astype(BF16)
    return _dot(m01, hi) + _dot(m01, lo)


def _ffn_kernel(*refs, final_norm, with_mix):
    if with_mix:
        x_ref, a_ref, b_ref, wm_ref, g_ref, wi_ref, wo_ref, fg_ref, o_ref = refs
    else:
        x_ref, g_ref, wi_ref, wo_ref, fg_ref, o_ref = refs
    d_ff = wo_ref.shape[0]
    x = x_ref[...]
    if with_mix:
        ka = a_ref.shape[1]
        x = x + _dot(a_ref[...], wm_ref[:ka, :]) + _dot(b_ref[...], wm_ref[ka:, :])
    n = _rms(x, g_ref[...]).astype(BF16)
    gate = _dot(n, wi_ref[:, :d_ff])
    up = _dot(n, wi_ref[:, d_ff:])
    act = (gate * jax.nn.sigmoid(gate) * up).astype(BF16)
    h = x + 0.5 * _dot(act, wo_ref[...])
    if final_norm:
        h = _rms(h, fg_ref[...])
    o_ref[...] = h


FFN_TM = 512


def _ffn(h, g, w_in, w_out, final_g, *, final_norm, mix=None):
    m, d = h.shape
    d_ff = w_out.shape[0]
    tm = min(FFN_TM, m)
    rows = lambda width: pl.BlockSpec((tm, width), lambda i: (i, 0))
    resident = lambda shape: pl.BlockSpec(shape, lambda i: (0, 0), pipeline_mode=pl.Buffered(1))
    operands, in_specs = [h], [rows(d)]
    if mix is not None:
        a, b, w_mix = mix
        operands += [a, b, w_mix]
        in_specs += [rows(a.shape[1]), rows(b.shape[1]), resident(w_mix.shape)]
    operands += [g.reshape(1, d), w_in, w_out, final_g.reshape(1, d)]
    in_specs += [resident((1, d)), resident((d, 2 * d_ff)), resident((d_ff, d)), resident((1, d))]
    return pl.pallas_call(
        functools.partial(_ffn_kernel, final_norm=final_norm, with_mix=mix is not None),
        grid=(m // tm,),
        in_specs=in_specs,
        out_specs=rows(d),
        out_shape=jax.ShapeDtypeStruct((m, d), F32),
        compiler_params=_params(("parallel",)),
        name="ffn",
    )(*operands)


def _proj_in_kernel(x_ref, g_ref, w_ref, o_ref, *wide_refs, wide_cols):
    y = _dot(_rms(x_ref[...], g_ref[...]).astype(BF16), w_ref[...])
    o_ref[...] = y.astype(o_ref.dtype)
    if wide_cols is not None:
        wide_refs[0][...] = y[:, wide_cols[0]:wide_cols[1]]


PROJ_TM = 512


def _proj_in(h, g, w, wide_cols=None):
    m, d = h.shape
    n_out = w.shape[1]
    tm = min(PROJ_TM, m)
    resident = lambda shape: pl.BlockSpec(shape, lambda i: (0, 0), pipeline_mode=pl.Buffered(1))
    out_specs = [pl.BlockSpec((tm, n_out), lambda i: (i, 0))]
    out_shape = [jax.ShapeDtypeStruct((m, n_out), BF16)]
    if wide_cols is not None:
        out_specs.append(pl.BlockSpec((tm, wide_cols[1] - wide_cols[0]), lambda i: (i, 0)))
        out_shape.append(jax.ShapeDtypeStruct((m, wide_cols[1] - wide_cols[0]), F32))
    return pl.pallas_call(
        functools.partial(_proj_in_kernel, wide_cols=wide_cols),
        grid=(m // tm,),
        in_specs=[
            pl.BlockSpec((tm, d), lambda i: (i, 0)),
            resident((1, d)),
            resident((d, n_out)),
        ],
        out_specs=out_specs,
        out_shape=out_shape,
        compiler_params=_params(("parallel",)),
        name="proj_in",
    )(h, g.reshape(1, d), w)


def _swa_kernel(sink_ref, q_ref, k_ref, v_ref, o_ref, k_swapped, v_swapped):
    nb = q_ref.shape[1] // BLK
    half = HEAD_DIM
    k_swapped[...] = pltpu.roll(k_ref[0].astype(F32), half, 1).astype(BF16)
    v_swapped[...] = pltpu.roll(v_ref[0].astype(F32), half, 1).astype(BF16)
    qi = lax.broadcasted_iota(jnp.int32, (BLK, 2 * BLK), 0)
    kj = lax.broadcasted_iota(jnp.int32, (BLK, 2 * BLK), 1)
    dist = qi + BLK - kj
    in_window = (dist >= 0) & (dist < WINDOW)
    dist_f = dist.astype(F32)
    lane = lax.broadcasted_iota(jnp.int32, (BLK, BLK), 1)
    ones = jnp.ones((2 * BLK, BLK), BF16)
    scale = HEAD_DIM ** -0.5

    def q_block(n, _):
        cur = pl.ds(pl.multiple_of(n * BLK, BLK), BLK)
        prev = pl.ds(pl.multiple_of(jnp.maximum(n - 1, 0) * BLK, BLK), BLK)
        valid = in_window & (n * BLK - BLK + kj >= 0)
        penalty = jnp.where(valid, dist_f, jnp.inf)
        band = lambda ref: jnp.concatenate([ref[prev, :], ref[cur, :]], axis=0)
        k_by_half = (band(k_ref.at[0]), band(k_swapped))
        v_by_half = (band(v_ref.at[0]), band(v_swapped))
        heads = range(SWA_HEADS)
        kv_half = [(h // SWA_GROUP) ^ (h % 2) for h in heads]
        scores = []
        for h in heads:
            q_col = q_ref[0, cur, (h // 2) * BLK:(h // 2 + 1) * BLK].astype(F32) * scale
            in_half = (lane >= (h % 2) * half) & (lane < (h % 2 + 1) * half)
            qm = jnp.where(in_half, q_col, 0.0).astype(BF16)
            slope = 2.0 ** (-8.0 * (h + 1) / SWA_HEADS)
            scores.append(_dot_nt(qm, k_by_half[kv_half[h]]) - slope * penalty)
        row_max = [jnp.maximum(jnp.max(scores[h], axis=-1, keepdims=True), sink_ref[h]) for h in heads]
        probs = [jnp.exp(scores[h] - row_max[h]).astype(BF16) for h in heads]
        num = [_dot(probs[h], v_by_half[kv_half[h]]) for h in heads]
        den = [_dot(probs[h], ones) + jnp.exp(sink_ref[h] - row_max[h]) for h in heads]
        out = [num[h] / den[h] for h in heads]
        for col in range(SWA_Q // BLK):
            o_ref[0, cur, col * BLK:(col + 1) * BLK] = jnp.where(
                lane < half, out[2 * col], out[2 * col + 1]).astype(o_ref.dtype)
        return 0

    lax.fori_loop(0, nb, q_block, 0)


def _swa(proj, sinks):
    b, s, _ = proj.shape
    assert SWA_KV == BLK and SWA_KV_HEADS == 2
    q_col = 3 * SB_W // SWA_Q
    k_col = (3 * SB_W + SWA_Q) // SWA_KV
    v_col = k_col + 1
    return pl.pallas_call(
        _swa_kernel,
        grid=(b,),
        in_specs=[
            pl.BlockSpec(memory_space=pltpu.SMEM),
            pl.BlockSpec((1, s, SWA_Q), lambda i: (i, 0, q_col)),
            pl.BlockSpec((1, s, SWA_KV), lambda i: (i, 0, k_col)),
            pl.BlockSpec((1, s, SWA_KV), lambda i: (i, 0, v_col)),
        ],
        out_specs=pl.BlockSpec((1, s, SWA_Q), lambda i: (i, 0, 0)),
        out_shape=jax.ShapeDtypeStruct((b, s, SWA_Q), BF16),
        scratch_shapes=[pltpu.VMEM((s, SWA_KV), BF16), pltpu.VMEM((s, SWA_KV), BF16)],
        compiler_params=_params(("parallel",)),
        name="swa",
    )(sinks, proj, proj, proj)


SB_TQ = 256
SB_KPI = 2
LOG2E = math.log2(math.e)


SB_PAIRS = 4


def _sb_kernel(q_ref, k_ref, v_ref, o_ref, kk_scr, vv_scr, carry_scr, acc_scr):
    s = q_ref.shape[1]
    nt = s // SB_TQ
    nb = s // BLK
    kpt = SB_TQ // BLK
    lane = lax.broadcasted_iota(jnp.int32, (BLK, BLK), 1)
    key_pos = lax.broadcasted_iota(jnp.int32, (SB_TQ, 2 * BLK), 1) & (BLK - 1)
    q_pos = lax.broadcasted_iota(jnp.int32, (SB_TQ, 2 * BLK), 0)
    jj = lax.broadcasted_iota(jnp.int32, (2 * BLK, 2 * BLK), 0) & (BLK - 1)
    ss = lax.broadcasted_iota(jnp.int32, (2 * BLK, 2 * BLK), 1)
    suffix_mat = jnp.where((jj > ss) | (ss >= BLK), 1.0, 0.0).astype(BF16)
    scale = HEAD_DIM ** -0.5

    def prep(j, _):
        rows = pl.ds(pl.multiple_of(j * BLK, BLK), BLK)
        for p in range(SB_PAIRS):
            kb = k_ref[0, rows, p * BLK:(p + 1) * BLK]
            vb = v_ref[0, rows, p * BLK:(p + 1) * BLK]
            kk_scr[p, j] = jnp.concatenate([jnp.where(lane < HEAD_DIM, kb, 0.0),
                                            jnp.where(lane >= HEAD_DIM, kb, 0.0)], axis=0).astype(BF16)
            vv_scr[p, j] = jnp.concatenate([jnp.where(lane < HEAD_DIM, vb, 0.0),
                                            jnp.where(lane >= HEAD_DIM, vb, 0.0)], axis=0).astype(BF16)
        return 0

    lax.fori_loop(0, nb, prep, 0)

    def key_blocks(qn, t, js, masked):
        chains = [(p, d) for p in range(SB_PAIRS) for d in range(len(js))]
        nz = {c: _dot_nt(qn[c[0]], kk_scr[c[0], js[c[1]]]) for c in chains}
        log_beta, lhs = {}, {}
        for c in chains:
            log_keep = jnp.minimum(nz[c], 0.0) - jnp.log2(1.0 + jnp.exp2(-jnp.abs(nz[c])))
            log_beta[c] = log_keep - nz[c]
            if masked:
                log_keep = jnp.where(causal_mask(t, js[c[1]]), log_keep, 0.0)
            hi = log_keep.astype(BF16)
            lo = (log_keep - hi.astype(F32)).astype(BF16)
            lhs[c] = [jnp.concatenate([hi[:, h * BLK:(h + 1) * BLK], lo[:, h * BLK:(h + 1) * BLK]], axis=1)
                      for h in range(2)]
        cs = {c: [_dot(lhs[c][h], suffix_mat) for h in range(2)] for c in chains}
        w = {}
        for p in range(SB_PAIRS):
            carry = carry_scr[p]
            for d, j in enumerate(js):
                c = (p, d)
                between = jnp.concatenate([cs[c][0][:, :BLK], cs[c][1][:, :BLK]], axis=1)
                wc = jnp.exp2(log_beta[c] + between + carry)
                if masked:
                    wc = jnp.where(causal_mask(t, j), wc, 0.0)
                w[c] = wc.astype(BF16)
                carry = carry + jnp.concatenate([cs[c][0][:, BLK:], cs[c][1][:, BLK:]], axis=1)
            carry_scr[p] = carry
        for p in range(SB_PAIRS):
            acc = acc_scr[p]
            for d, j in enumerate(js):
                acc = acc + _dot(w[(p, d)], vv_scr[p, j])
            acc_scr[p] = acc

    def causal_mask(t, j):
        return (j * BLK + key_pos) < (t * SB_TQ + q_pos)

    def q_tile(t, _):
        rows = pl.ds(pl.multiple_of(t * SB_TQ, SB_TQ), SB_TQ)
        qn = [(q_ref[0, rows, p * BLK:(p + 1) * BLK].astype(F32) * (-scale * LOG2E)).astype(BF16)
              for p in range(SB_PAIRS)]
        carry_scr[...] = jnp.zeros_like(carry_scr)
        acc_scr[...] = jnp.zeros_like(acc_scr)
        key_blocks(qn, t, [kpt * t + kpt - 1 - d for d in range(kpt)], True)

        def kv_step(i, _):
            key_blocks(qn, t, [kpt * t - 1 - i * SB_KPI - d for d in range(SB_KPI)], False)
            return 0

        lax.fori_loop(0, kpt * t // SB_KPI, kv_step, 0)
        for p in range(SB_PAIRS):
            o_ref[0, rows, p * BLK:(p + 1) * BLK] = acc_scr[p].astype(o_ref.dtype)
        return 0

    lax.fori_loop(0, nt, q_tile, 0)


def _sb(proj):
    b, s, _ = proj.shape
    assert (SB_TQ // BLK) % SB_KPI == 0
    width = SB_PAIRS * BLK
    q_col = 0
    k_col = SB_W // width
    v_col = 2 * SB_W // width
    return pl.pallas_call(
        _sb_kernel,
        grid=(b, SB_W // width),
        in_specs=[
            pl.BlockSpec((1, s, width), lambda i, p: (i, 0, q_col + p)),
            pl.BlockSpec((1, s, width), lambda i, p: (i, 0, k_col + p)),
            pl.BlockSpec((1, s, width), lambda i, p: (i, 0, v_col + p)),
        ],
        out_specs=pl.BlockSpec((1, s, width), lambda i, p: (i, 0, p)),
        out_shape=jax.ShapeDtypeStruct((b, s, SB_W), BF16),
        scratch_shapes=[pltpu.VMEM((SB_PAIRS, s // BLK, 2 * BLK, BLK), BF16),
                        pltpu.VMEM((SB_PAIRS, s // BLK, 2 * BLK, BLK), BF16),
                        pltpu.VMEM((SB_PAIRS, SB_TQ, 2 * BLK), F32), pltpu.VMEM((SB_PAIRS, SB_TQ, BLK), F32)],
        compiler_params=_params(("parallel", "parallel")),
        name="stick_breaking",
    )(proj, proj, proj)


def _hgrn_kernel(layer, lbl_ref, ng_ref, q_ref, f_ref, i_ref, gate_ref, o_ref, state_scr):
    nc = q_ref.shape[1] // HG_CHUNK
    logits = lbl_ref[...]
    e = jnp.exp(logits - jnp.max(logits, axis=0, keepdims=True))
    sm = e / jnp.sum(e, axis=0, keepdims=True)
    lb = jnp.sum(sm[1:layer + 1], axis=0, keepdims=True)
    ti = lax.broadcasted_iota(jnp.int32, (HG_CHUNK, HG_CHUNK), 0)
    si = lax.broadcasted_iota(jnp.int32, (HG_CHUNK, HG_CHUNK), 1)
    cumsum_mat = jnp.where(si <= ti, 1.0, 0.0).astype(BF16)
    row8 = lax.broadcasted_iota(jnp.int32, (SUBLANES, 1), 0)
    n_sub = HG_CHUNK // HG_SUB
    state_scr[...] = jnp.zeros_like(state_scr)

    def pair_rows(acc, qt, gt, ks, gs, vs, s, masked):
        decay = jnp.exp2(gt - gs[s:s + 1, :])
        col = jnp.sum(qt * (ks[s:s + 1, :] * decay), axis=-1, keepdims=True)
        if masked:
            col = jnp.where(row8 >= s, col, 0.0)
        return acc + col * vs[s:s + 1, :]

    def chunk(c, _):
        rows = pl.ds(pl.multiple_of(c * HG_CHUNK, HG_CHUNK), HG_CHUNK)
        z = f_ref[0, rows, :]
        qx = q_ref[0, rows, :].astype(F32)
        v_all = i_ref[0, rows, :].astype(F32)
        gate = gate_ref[0, rows, :].astype(F32)
        sig = jax.nn.sigmoid(z)
        q_all = qx * jax.nn.sigmoid(qx)
        k_all = (1.0 - lb) * (1.0 - sig)
        log2_f = jnp.log2(lb + (1.0 - lb) * sig)
        g_all = _split_dot_left(cumsum_mat, log2_f)
        g_last_all = g_all[HG_CHUNK - 1:HG_CHUNK, :]
        qg_all = (q_all * jnp.exp2(g_all)).astype(BF16)
        kd_all = (k_all * jnp.exp2(g_last_all - g_all)).astype(BF16)
        gated = gate * jax.nn.sigmoid(gate)
        kt_all, qt_all = [], []
        for j in range(n_sub - 1):
            lo, hi = j * HG_SUB, (j + 1) * HG_SUB
            r = g_all[hi - 1:hi, :]
            kt_all.append((k_all[lo:hi] * jnp.exp2(r - g_all[lo:hi])).astype(BF16))
            qt_all.append((q_all[hi:] * jnp.exp2(g_all[hi:] - r)).astype(BF16))
        heads = range(HG_HEADS)
        cols = [slice(h * HG_DK, (h + 1) * HG_DK) for h in heads]
        v_bf = v_all.astype(BF16)
        state = [state_scr[h] for h in heads]
        o_inter = [_dot_nt(qg_all[:, cols[h]], state[h].astype(BF16)) for h in heads]
        sc = [[_dot_nt(qt_all[j][:, cols[h]], kt_all[j][:, cols[h]]).astype(BF16)
               for j in range(n_sub - 1)] for h in heads]
        contrib = [[_dot(sc[h][j], v_bf[j * HG_SUB:(j + 1) * HG_SUB, cols[h]])
                    for j in range(n_sub - 1)] for h in heads]
        new_state = [state[h] * jnp.exp2(g_last_all[:, cols[h]]) + _dot_tn(v_bf[:, cols[h]], kd_all[:, cols[h]])
                     for h in heads]
        tiles = [[None] * (HG_CHUNK // SUBLANES) for _ in heads]
        for h in heads:
            for i in range(n_sub):
                part = jnp.zeros((HG_SUB, HG_DK), F32)
                for j in range(i):
                    part = part + contrib[h][j][(i - j - 1) * HG_SUB:(i - j) * HG_SUB]
                tiles[h][2 * i], tiles[h][2 * i + 1] = part[:SUBLANES], part[SUBLANES:]
        for i in range(n_sub):
            lo, mid, hi = i * HG_SUB, i * HG_SUB + SUBLANES, (i + 1) * HG_SUB
            a, b_ = slice(lo, mid), slice(mid, hi)
            for s in range(SUBLANES):
                for h in heads:
                    g, q, k, v = g_all[:, cols[h]], q_all[:, cols[h]], k_all[:, cols[h]], v_all[:, cols[h]]
                    t = tiles[h]
                    t[2 * i] = pair_rows(t[2 * i], q[a], g[a], k[a], g[a], v[a], s, True)
                    t[2 * i + 1] = pair_rows(t[2 * i + 1], q[b_], g[b_], k[a], g[a], v[a], s, False)
                    t[2 * i + 1] = pair_rows(t[2 * i + 1], q[b_], g[b_], k[b_], g[b_], v[b_], s, True)
        for h in heads:
            o = o_inter[h] + jnp.concatenate(tiles[h], axis=0)
            state_scr[h] = new_state[h]
            o_ref[0, rows, cols[h]] = (_rms(o, ng_ref[...]) * gated[:, cols[h]]).astype(o_ref.dtype)
        return 0

    lax.fori_loop(0, nc, chunk, 0)


def _hgrn(proj, f_logit, lb_logits, norm_g, layer):
    b, s, _ = proj.shape
    depth = lb_logits.shape[0]
    assert HG_SUB == 2 * SUBLANES
    blk = lambda off: pl.BlockSpec((1, s, HG_WIDTH), lambda i: (i, 0, off))
    return pl.pallas_call(
        functools.partial(_hgrn_kernel, layer),
        grid=(b,),
        in_specs=[
            pl.BlockSpec((depth, HG_WIDTH), lambda i: (0, 0)),
            pl.BlockSpec((1, HG_DK), lambda i: (0, 0)),
            blk(0), blk(0), blk(2), blk(3),
        ],
        out_specs=pl.BlockSpec((1, s, HG_WIDTH), lambda i: (i, 0, 0)),
        out_shape=jax.ShapeDtypeStruct((b, s, HG_WIDTH), BF16),
        scratch_shapes=[pltpu.VMEM((HG_HEADS, HG_DK, HG_DK), F32)],
        compiler_params=_params(("parallel",)),
        name="hgrn2",
    )(lb_logits, norm_g.reshape(1, HG_DK), proj, f_logit, proj, proj)


CONV_HALO = 16


def _conv_kernel(w_ref, gb_ref, gc_ref, u_ref, gch_ref, uh_ref, o_ref):
    first = pl.program_id(1) == 0
    x = gc_ref[0].astype(F32) * u_ref[0].astype(F32)
    halo = jnp.where(first, 0.0, gch_ref[0].astype(F32) * uh_ref[0].astype(F32))
    row = lax.broadcasted_iota(jnp.int32, x.shape, 0)
    h1 = halo[CONV_HALO - 1:CONV_HALO, :]
    h2 = halo[CONV_HALO - 2:CONV_HALO - 1, :]
    x1 = jnp.where(row == 0, h1, pltpu.roll(x, 1, 0))
    x2 = jnp.where(row == 0, h2, jnp.where(row == 1, h1, pltpu.roll(x, 2, 0)))
    y = w_ref[0:1, :] * x2 + w_ref[1:2, :] * x1 + w_ref[2:3, :] * x
    o_ref[0] = (gb_ref[0].astype(F32) * y).astype(o_ref.dtype)


def _conv(proj, w):
    b, s, _ = proj.shape
    ts = min(512, s)
    c0 = 4 * HG_WIDTH // CONV_WIDTH
    main = lambda off: pl.BlockSpec((1, ts, CONV_WIDTH), lambda i, t: (i, t, c0 + off))
    halo = lambda off: pl.BlockSpec(
        (1, CONV_HALO, CONV_WIDTH),
        lambda i, t: (i, jnp.maximum(t * (ts // CONV_HALO) - 1, 0), c0 + off))
    return pl.pallas_call(
        _conv_kernel,
        grid=(b, s // ts),
        in_specs=[
            pl.BlockSpec((CONV_K, CONV_WIDTH), lambda i, t: (0, 0)),
            main(0), main(1), main(2), halo(1), halo(2),
        ],
        out_specs=pl.BlockSpec((1, ts, CONV_WIDTH), lambda i, t: (i, t, 0)),
        out_shape=jax.ShapeDtypeStruct((b, s, CONV_WIDTH), BF16),
        compiler_params=_params(("parallel", "arbitrary")),
        name="short_conv",
    )(w, proj, proj, proj, proj, proj)


def _attn_columns(w_in):
    split = SWA_Q + 2 * SWA_KV
    return jnp.concatenate([w_in[:, split:], w_in[:, :split]], axis=1)


def kernel(x, norm_g, ffn_w_in, ffn_w_out, attn_w_in, attn_sinks, attn_w_out, rec_w_in,
           hgrn_lb_logits, hgrn_norm_g, conv_w, rec_w_out, final_g):
    b, s, d = x.shape
    depth = norm_g.shape[0]
    m = b * s
    h = x.reshape(m, d)
    for layer in range(depth):
        h = _ffn(h, norm_g[layer, 0], ffn_w_in[layer, 0].astype(BF16), ffn_w_out[layer, 0].astype(BF16),
                 final_g, final_norm=False)
        if layer % 2 == 0:
            e = layer // 2
            proj = _proj_in(h, norm_g[layer, 1], _attn_columns(attn_w_in[e]).astype(BF16))[0]
            proj = proj.reshape(b, s, ATTN_IN)
            o_a = _swa(proj, attn_sinks[e])
            o_b = _sb(proj)
            mix = (o_a.reshape(m, SWA_Q), o_b.reshape(m, SB_W), attn_w_out[e].astype(BF16))
        else:
            o = layer // 2
            proj, f_logit = _proj_in(h, norm_g[layer, 1], rec_w_in[o].astype(BF16), (HG_WIDTH, 2 * HG_WIDTH))
            proj = proj.reshape(b, s, REC_IN)
            o_c = _hgrn(proj, f_logit.reshape(b, s, HG_WIDTH), hgrn_lb_logits, hgrn_norm_g[o], layer)
            o_d = _conv(proj, conv_w[o])
            mix = (o_c.reshape(m, HG_WIDTH), o_d.reshape(m, CONV_WIDTH), rec_w_out[o].astype(BF16))
        h = _ffn(h, norm_g[layer, 2], ffn_w_in[layer, 1].astype(BF16), ffn_w_out[layer, 1].astype(BF16),
                 final_g, final_norm=(layer == depth - 1), mix=mix)
    return h.reshape(b, s, d)
```

```python
import functools
import math

import jax
import jax.numpy as jnp
from jax import lax
from jax.experimental import pallas as pl
from jax.experimental.pallas import tpu as pltpu

F32 = jnp.float32
BF16 = jnp.bfloat16

D_MODEL = 1024
HEAD_DIM = 64
SWA_HEADS = 8
SWA_KV_HEADS = 2
SWA_GROUP = SWA_HEADS // SWA_KV_HEADS
WINDOW = 128
BLK = 128
SUBLANES = 8
SB_HEADS = 8
HG_WIDTH = 512
HG_DK = 128
HG_HEADS = HG_WIDTH // HG_DK
HG_CHUNK = 64
HG_SUB = 16
CONV_WIDTH = 512
CONV_K = 3
D_FF = 2816
EPS = 1e-6

SWA_Q = SWA_HEADS * HEAD_DIM
SWA_KV = SWA_KV_HEADS * HEAD_DIM
SB_W = SB_HEADS * HEAD_DIM
ATTN_IN = SWA_Q + 2 * SWA_KV + 3 * SB_W
REC_IN = 4 * HG_WIDTH + 3 * CONV_WIDTH

VMEM_LIMIT_BYTES = 56 * 1024 * 1024


def _params(semantics):
    return pltpu.CompilerParams(dimension_semantics=semantics,
                                vmem_limit_bytes=VMEM_LIMIT_BYTES)


def _rms(x, g):
    return x * lax.rsqrt(jnp.mean(x * x, axis=-1, keepdims=True) + EPS) * g


def _dot(a, b):
    return jnp.dot(a, b, preferred_element_type=F32)


def _dot_nt(a, b):
    return lax.dot_general(a, b, (((1,), (1,)), ((), ())), preferred_element_type=F32)


def _dot_tn(a, b):
    return lax.dot_general(a, b, (((0,), (0,)), ((), ())), preferred_element_type=F32)


def _split_dot(x, m01):
    hi = x.astype(BF16)
    lo = (x - hi.astype(F32)).astype(BF16)
    return _dot(hi, m01) + _dot(lo, m01)


def _split_dot_left(m01, x):
    hi = x.astype(BF16)
    lo = (x - hi.astype(F32)).astype(BF16)
    return _dot(m01, hi) + _dot(m01, lo)


def _ffn_kernel(*refs, final_norm, with_mix):
    if with_mix:
        x_ref, a_ref, b_ref, wm_ref, g_ref, wi_ref, wo_ref, fg_ref, o_ref = refs
    else:
        x_ref, g_ref, wi_ref, wo_ref, fg_ref, o_ref = refs
    d_ff = wo_ref.shape[0]
    x = x_ref[...]
    if with_mix:
        ka = a_ref.shape[1]
        x = x + _dot(a_ref[...], wm_ref[:ka, :]) + _dot(b_ref[...], wm_ref[ka:, :])
    n = _rms(x, g_ref[...]).astype(BF16)
    gate = _dot(n, wi_ref[:, :d_ff])
    up = _dot(n, wi_ref[:, d_ff:])
    act = (gate * jax.nn.sigmoid(gate) * up).astype(BF16)
    h = x + 0.5 * _dot(act, wo_ref[...])
    if final_norm:
        h = _rms(h, fg_ref[...])
    o_ref[...] = h


FFN_TM = 512


def _ffn(h, g, w_in, w_out, final_g, *, final_norm, mix=None):
    m, d = h.shape
    d_ff = w_out.shape[0]
    tm = min(FFN_TM, m)
    rows = lambda width: pl.BlockSpec((tm, width), lambda i: (i, 0))
    resident = lambda shape: pl.BlockSpec(shape, lambda i: (0, 0), pipeline_mode=pl.Buffered(1))
    operands, in_specs = [h], [rows(d)]
    if mix is not None:
        a, b, w_mix = mix
        operands += [a, b, w_mix]
        in_specs += [rows(a.shape[1]), rows(b.shape[1]), resident(w_mix.shape)]
    operands += [g.reshape(1, d), w_in, w_out, final_g.reshape(1, d)]
    in_specs += [resident((1, d)), resident((d, 2 * d_ff)), resident((d_ff, d)), resident((1, d))]
    return pl.pallas_call(
        functools.partial(_ffn_kernel, final_norm=final_norm, with_mix=mix is not None),
        grid=(m // tm,),
        in_specs=in_specs,
        out_specs=rows(d),
        out_shape=jax.ShapeDtypeStruct((m, d), F32),
        compiler_params=_params(("parallel",)),
        name="ffn",
    )(*operands)


def _proj_in_kernel(x_ref, g_ref, w_ref, o_ref, *wide_refs, wide_cols):
    y = _dot(_rms(x_ref[...], g_ref[...]).astype(BF16), w_ref[...])
    o_ref[...] = y.astype(o_ref.dtype)
    if wide_cols is not None:
        wide_refs[0][...] = y[:, wide_cols[0]:wide_cols[1]]


PROJ_TM = 512


def _proj_in(h, g, w, wide_cols=None):
    m, d = h.shape
    n_out = w.shape[1]
    tm = min(PROJ_TM, m)
    resident = lambda shape: pl.BlockSpec(shape, lambda i: (0, 0), pipeline_mode=pl.Buffered(1))
    out_specs = [pl.BlockSpec((tm, n_out), lambda i: (i, 0))]
    out_shape = [jax.ShapeDtypeStruct((m, n_out), BF16)]
    if wide_cols is not None:
        out_specs.append(pl.BlockSpec((tm, wide_cols[1] - wide_cols[0]), lambda i: (i, 0)))
        out_shape.append(jax.ShapeDtypeStruct((m, wide_cols[1] - wide_cols[0]), F32))
    return pl.pallas_call(
        functools.partial(_proj_in_kernel, wide_cols=wide_cols),
        grid=(m // tm,),
        in_specs=[
            pl.BlockSpec((tm, d), lambda i: (i, 0)),
            resident((1, d)),
            resident((d, n_out)),
        ],
        out_specs=out_specs,
        out_shape=out_shape,
        compiler_params=_params(("parallel",)),
        name="proj_in",
    )(h, g.reshape(1, d), w)


def _swa_kernel(sink_ref, q_ref, k_ref, v_ref, o_ref, k_swapped, v_swapped):
    nb = q_ref.shape[1] // BLK
    half = HEAD_DIM
    k_swapped[...] = pltpu.roll(k_ref[0].astype(F32), half, 1).astype(BF16)
    v_swapped[...] = pltpu.roll(v_ref[0].astype(F32), half, 1).astype(BF16)
    qi = lax.broadcasted_iota(jnp.int32, (BLK, 2 * BLK), 0)
    kj = lax.broadcasted_iota(jnp.int32, (BLK, 2 * BLK), 1)
    dist = qi + BLK - kj
    in_window = (dist >= 0) & (dist < WINDOW)
    dist_f = dist.astype(F32)
    lane = lax.broadcasted_iota(jnp.int32, (BLK, BLK), 1)
    ones = jnp.ones((2 * BLK, BLK), BF16)
    scale = HEAD_DIM ** -0.5

    def q_block(n, _):
        cur = pl.ds(pl.multiple_of(n * BLK, BLK), BLK)
        prev = pl.ds(pl.multiple_of(jnp.maximum(n - 1, 0) * BLK, BLK), BLK)
        valid = in_window & (n * BLK - BLK + kj >= 0)
        penalty = jnp.where(valid, dist_f, jnp.inf)
        band = lambda ref: jnp.concatenate([ref[prev, :], ref[cur, :]], axis=0)
        k_by_half = (band(k_ref.at[0]), band(k_swapped))
        v_by_half = (band(v_ref.at[0]), band(v_swapped))
        heads = range(SWA_HEADS)
        kv_half = [(h // SWA_GROUP) ^ (h % 2) for h in heads]
        scores = []
        for h in heads:
            q_col = q_ref[0, cur, (h // 2) * BLK:(h // 2 + 1) * BLK].astype(F32) * scale
            in_half = (lane >= (h % 2) * half) & (lane < (h % 2 + 1) * half)
            qm = jnp.where(in_half, q_col, 0.0).astype(BF16)
            slope = 2.0 ** (-8.0 * (h + 1) / SWA_HEADS)
            scores.append(_dot_nt(qm, k_by_half[kv_half[h]]) - slope * penalty)
        row_max = [jnp.maximum(jnp.max(scores[h], axis=-1, keepdims=True), sink_ref[h]) for h in heads]
        probs = [jnp.exp(scores[h] - row_max[h]).astype(BF16) for h in heads]
        num = [_dot(probs[h], v_by_half[kv_half[h]]) for h in heads]
        den = [_dot(probs[h], ones) + jnp.exp(sink_ref[h] - row_max[h]) for h in heads]
        out = [num[h] / den[h] for h in heads]
        for col in range(SWA_Q // BLK):
            o_ref[0, cur, col * BLK:(col + 1) * BLK] = jnp.where(
                lane < half, out[2 * col], out[2 * col + 1]).astype(o_ref.dtype)
        return 0

    lax.fori_loop(0, nb, q_block, 0)


def _swa(proj, sinks):
    b, s, _ = proj.shape
    assert SWA_KV == BLK and SWA_KV_HEADS == 2
    q_col = 3 * SB_W // SWA_Q
    k_col = (3 * SB_W + SWA_Q) // SWA_KV
    v_col = k_col + 1
    return pl.pallas_call(
        _swa_kernel,
        grid=(b,),
        in_specs=[
            pl.BlockSpec(memory_space=pltpu.SMEM),
            pl.BlockSpec((1, s, SWA_Q), lambda i: (i, 0, q_col)),
            pl.BlockSpec((1, s, SWA_KV), lambda i: (i, 0, k_col)),
            pl.BlockSpec((1, s, SWA_KV), lambda i: (i, 0, v_col)),
        ],
        out_specs=pl.BlockSpec((1, s, SWA_Q), lambda i: (i, 0, 0)),
        out_shape=jax.ShapeDtypeStruct((b, s, SWA_Q), BF16),
        scratch_shapes=[pltpu.VMEM((s, SWA_KV), BF16), pltpu.VMEM((s, SWA_KV), BF16)],
        compiler_params=_params(("parallel",)),
        name="swa",
    )(sinks, proj, proj, proj)


SB_TQ = 256
SB_KPI = 2
LOG2E = math.log2(math.e)
SB_DEAD = -256.0


SB_PAIRS = 4


def _sb_kernel(q_ref, k_ref, v_ref, o_ref, kk_scr, vv_scr, carry_scr, acc_scr):
    s = q_ref.shape[1]
    nt = s // SB_TQ
    nb = s // BLK
    kpt = SB_TQ // BLK
    lane = lax.broadcasted_iota(jnp.int32, (BLK, BLK), 1)
    key_pos = lax.broadcasted_iota(jnp.int32, (SB_TQ, 2 * BLK), 1) & (BLK - 1)
    q_pos = lax.broadcasted_iota(jnp.int32, (SB_TQ, 2 * BLK), 0)
    jj = lax.broadcasted_iota(jnp.int32, (2 * BLK, 2 * BLK), 0) & (BLK - 1)
    ss = lax.broadcasted_iota(jnp.int32, (2 * BLK, 2 * BLK), 1)
    suffix_mat = jnp.where((jj > ss) | (ss >= BLK), 1.0, 0.0).astype(BF16)
    scale = HEAD_DIM ** -0.5

    def prep(j, _):
        rows = pl.ds(pl.multiple_of(j * BLK, BLK), BLK)
        for p in range(SB_PAIRS):
            kb = k_ref[0, rows, p * BLK:(p + 1) * BLK]
            vb = v_ref[0, rows, p * BLK:(p + 1) * BLK]
            kk_scr[p, j] = jnp.concatenate([jnp.where(lane < HEAD_DIM, kb, 0.0),
                                            jnp.where(lane >= HEAD_DIM, kb, 0.0)], axis=0).astype(BF16)
            vv_scr[p, j] = jnp.concatenate([jnp.where(lane < HEAD_DIM, vb, 0.0),
                                            jnp.where(lane >= HEAD_DIM, vb, 0.0)], axis=0).astype(BF16)
        return 0

    lax.fori_loop(0, nb, prep, 0)

    def key_blocks(qn, t, js, masked):
        first_row = [(len(js) - 1 - d) * BLK if masked else 0 for d in range(len(js))]
        chains = [(p, d) for p in range(SB_PAIRS) for d in range(len(js))]
        causal = [((js[d] * BLK + key_pos) < (t * SB_TQ + q_pos))[first_row[d]:] if masked else None
                  for d in range(len(js))]
        nz = {(p, d): _dot_nt(qn[p][first_row[d]:], kk_scr[p, js[d]]) for p, d in chains}
        log_beta, lhs = {}, {}
        for c in chains:
            log_keep = jnp.minimum(nz[c], 0.0) - jnp.log2(1.0 + jnp.exp2(-jnp.abs(nz[c])))
            log_beta[c] = log_keep - nz[c]
            if masked:
                log_keep = jnp.where(causal[c[1]], log_keep, 0.0)
            hi = log_keep.astype(BF16)
            lo = (log_keep - hi.astype(F32)).astype(BF16)
            lhs[c] = [jnp.concatenate([hi[:, h * BLK:(h + 1) * BLK], lo[:, h * BLK:(h + 1) * BLK]], axis=1)
                      for h in range(2)]
        cs = {c: [_dot(lhs[c][h], suffix_mat) for h in range(2)] for c in chains}
        w = {}
        for p in range(SB_PAIRS):
            carry = carry_scr[p]
            for d in range(len(js)):
                c, r0 = (p, d), first_row[d]
                between = jnp.concatenate([cs[c][0][:, :BLK], cs[c][1][:, :BLK]], axis=1)
                wc = jnp.exp2(log_beta[c] + between + carry[r0:])
                if masked:
                    wc = jnp.where(causal[d], wc, 0.0)
                w[c] = wc.astype(BF16)
                below = carry[r0:] + jnp.concatenate([cs[c][0][:, BLK:], cs[c][1][:, BLK:]], axis=1)
                carry = below if r0 == 0 else jnp.concatenate([carry[:r0], below], axis=0)
            carry_scr[p] = carry
        for p in range(SB_PAIRS):
            acc = acc_scr[p]
            for d, j in enumerate(js):
                r0 = first_row[d]
                below = acc[r0:] + _dot(w[(p, d)], vv_scr[p, j])
                acc = below if r0 == 0 else jnp.concatenate([acc[:r0], below], axis=0)
            acc_scr[p] = acc

    def q_tile(t, _):
        rows = pl.ds(pl.multiple_of(t * SB_TQ, SB_TQ), SB_TQ)
        qn = [(q_ref[0, rows, p * BLK:(p + 1) * BLK].astype(F32) * (-scale * LOG2E)).astype(BF16)
              for p in range(SB_PAIRS)]
        carry_scr[...] = jnp.zeros_like(carry_scr)
        acc_scr[...] = jnp.zeros_like(acc_scr)
        key_blocks(qn, t, [kpt * t + kpt - 1 - d for d in range(kpt)], True)

        def live(state):
            return (state[0] < kpt * t // SB_KPI) & (state[1] > SB_DEAD)

        def kv_step(state):
            i = state[0]
            key_blocks(qn, t, [kpt * t - 1 - i * SB_KPI - d for d in range(SB_KPI)], False)
            return i + 1, jnp.max(carry_scr[...])

        lax.while_loop(live, kv_step, (jnp.int32(0), jnp.float32(0.0)))
        for p in range(SB_PAIRS):
            o_ref[0, rows, p * BLK:(p + 1) * BLK] = acc_scr[p].astype(o_ref.dtype)
        return 0

    lax.fori_loop(0, nt, q_tile, 0)


def _sb(proj):
    b, s, _ = proj.shape
    assert (SB_TQ // BLK) % SB_KPI == 0
    width = SB_PAIRS * BLK
    q_col = 0
    k_col = SB_W // width
    v_col = 2 * SB_W // width
    return pl.pallas_call(
        _sb_kernel,
        grid=(b, SB_W // width),
        in_specs=[
            pl.BlockSpec((1, s, width), lambda i, p: (i, 0, q_col + p)),
            pl.BlockSpec((1, s, width), lambda i, p: (i, 0, k_col + p)),
            pl.BlockSpec((1, s, width), lambda i, p: (i, 0, v_col + p)),
        ],
        out_specs=pl.BlockSpec((1, s, width), lambda i, p: (i, 0, p)),
        out_shape=jax.ShapeDtypeStruct((b, s, SB_W), BF16),
        scratch_shapes=[pltpu.VMEM((SB_PAIRS, s // BLK, 2 * BLK, BLK), BF16),
                        pltpu.VMEM((SB_PAIRS, s // BLK, 2 * BLK, BLK), BF16),
                        pltpu.VMEM((SB_PAIRS, SB_TQ, 2 * BLK), F32), pltpu.VMEM((SB_PAIRS, SB_TQ, BLK), F32)],
        compiler_params=_params(("parallel", "parallel")),
        name="stick_breaking",
    )(proj, proj, proj)


def _hgrn_kernel(layer, lbl_ref, ng_ref, q_ref, f_ref, i_ref, gate_ref, o_ref, state_scr):
    nc = q_ref.shape[1] // HG_CHUNK
    logits = lbl_ref[...]
    e = jnp.exp(logits - jnp.max(logits, axis=0, keepdims=True))
    sm = e / jnp.sum(e, axis=0, keepdims=True)
    lb = jnp.sum(sm[1:layer + 1], axis=0, keepdims=True)
    ti = lax.broadcasted_iota(jnp.int32, (HG_CHUNK, HG_CHUNK), 0)
    si = lax.broadcasted_iota(jnp.int32, (HG_CHUNK, HG_CHUNK), 1)
    cumsum_mat = jnp.where(si <= ti, 1.0, 0.0).astype(BF16)
    row8 = lax.broadcasted_iota(jnp.int32, (SUBLANES, 1), 0)
    n_sub = HG_CHUNK // HG_SUB
    state_scr[...] = jnp.zeros_like(state_scr)

    def pair_rows(acc, qt, gt, ks, gs, vs, s, masked):
        decay = jnp.exp2(gt - gs[s:s + 1, :])
        col = jnp.sum(qt * (ks[s:s + 1, :] * decay), axis=-1, keepdims=True)
        if masked:
            col = jnp.where(row8 >= s, col, 0.0)
        return acc + col * vs[s:s + 1, :]

    def chunk(c, _):
        rows = pl.ds(pl.multiple_of(c * HG_CHUNK, HG_CHUNK), HG_CHUNK)
        z = f_ref[0, rows, :]
        qx = q_ref[0, rows, :].astype(F32)
        v_all = i_ref[0, rows, :].astype(F32)
        gate = gate_ref[0, rows, :].astype(F32)
        sig = jax.nn.sigmoid(z)
        q_all = qx * jax.nn.sigmoid(qx)
        k_all = (1.0 - lb) * (1.0 - sig)
        log2_f = jnp.log2(lb + (1.0 - lb) * sig)
        g_all = _split_dot_left(cumsum_mat, log2_f)
        g_last_all = g_all[HG_CHUNK - 1:HG_CHUNK, :]
        qg_all = (q_all * jnp.exp2(g_all)).astype(BF16)
        kd_all = (k_all * jnp.exp2(g_last_all - g_all)).astype(BF16)
        gated = gate * jax.nn.sigmoid(gate)
        kt_all, qt_all = [], []
        for j in range(n_sub - 1):
            lo, hi = j * HG_SUB, (j + 1) * HG_SUB
            r = g_all[hi - 1:hi, :]
            kt_all.append((k_all[lo:hi] * jnp.exp2(r - g_all[lo:hi])).astype(BF16))
            qt_all.append((q_all[hi:] * jnp.exp2(g_all[hi:] - r)).astype(BF16))
        heads = range(HG_HEADS)
        cols = [slice(h * HG_DK, (h + 1) * HG_DK) for h in heads]
        v_bf = v_all.astype(BF16)
        state = [state_scr[h] for h in heads]
        o_inter = [_dot_nt(qg_all[:, cols[h]], state[h].astype(BF16)) for h in heads]
        sc = [[_dot_nt(qt_all[j][:, cols[h]], kt_all[j][:, cols[h]]).astype(BF16)
               for j in range(n_sub - 1)] for h in heads]
        contrib = [[_dot(sc[h][j], v_bf[j * HG_SUB:(j + 1) * HG_SUB, cols[h]])
                    for j in range(n_sub - 1)] for h in heads]
        new_state = [state[h] * jnp.exp2(g_last_all[:, cols[h]]) + _dot_tn(v_bf[:, cols[h]], kd_all[:, cols[h]])
                     for h in heads]
        tiles = [[None] * (HG_CHUNK // SUBLANES) for _ in heads]
        for h in heads:
            for i in range(n_sub):
                part = jnp.zeros((HG_SUB, HG_DK), F32)
                for j in range(i):
                    part = part + contrib[h][j][(i - j - 1) * HG_SUB:(i - j) * HG_SUB]
                tiles[h][2 * i], tiles[h][2 * i + 1] = part[:SUBLANES], part[SUBLANES:]
        for i in range(n_sub):
            lo, mid, hi = i * HG_SUB, i * HG_SUB + SUBLANES, (i + 1) * HG_SUB
            a, b_ = slice(lo, mid), slice(mid, hi)
            for s in range(SUBLANES):
                for h in heads:
                    g, q, k, v = g_all[:, cols[h]], q_all[:, cols[h]], k_all[:, cols[h]], v_all[:, cols[h]]
                    t = tiles[h]
                    t[2 * i] = pair_rows(t[2 * i], q[a], g[a], k[a], g[a], v[a], s, True)
                    t[2 * i + 1] = pair_rows(t[2 * i + 1], q[b_], g[b_], k[a], g[a], v[a], s, False)
                    t[2 * i + 1] = pair_rows(t[2 * i + 1], q[b_], g[b_], k[b_], g[b_], v[b_], s, True)
        for h in heads:
            o = o_inter[h] + jnp.concatenate(tiles[h], axis=0)
            state_scr[h] = new_state[h]
            o_ref[0, rows, cols[h]] = (_rms(o, ng_ref[...]) * gated[:, cols[h]]).astype(o_ref.dtype)
        return 0

    lax.fori_loop(0, nc, chunk, 0)


def _hgrn(proj, f_logit, lb_logits, norm_g, layer):
    b, s, _ = proj.shape
    depth = lb_logits.shape[0]
    assert HG_SUB == 2 * SUBLANES
    blk = lambda off: pl.BlockSpec((1, s, HG_WIDTH), lambda i: (i, 0, off))
    return pl.pallas_call(
        functools.partial(_hgrn_kernel, layer),
        grid=(b,),
        in_specs=[
            pl.BlockSpec((depth, HG_WIDTH), lambda i: (0, 0)),
            pl.BlockSpec((1, HG_DK), lambda i: (0, 0)),
            blk(0), blk(0), blk(2), blk(3),
        ],
        out_specs=pl.BlockSpec((1, s, HG_WIDTH), lambda i: (i, 0, 0)),
        out_shape=jax.ShapeDtypeStruct((b, s, HG_WIDTH), BF16),
        scratch_shapes=[pltpu.VMEM((HG_HEADS, HG_DK, HG_DK), F32)],
        compiler_params=_params(("parallel",)),
        name="hgrn2",
    )(lb_logits, norm_g.reshape(1, HG_DK), proj, f_logit, proj, proj)


CONV_HALO = 16


def _conv_kernel(w_ref, gb_ref, gc_ref, u_ref, gch_ref, uh_ref, o_ref):
    first = pl.program_id(1) == 0
    x = gc_ref[0].astype(F32) * u_ref[0].astype(F32)
    halo = jnp.where(first, 0.0, gch_ref[0].astype(F32) * uh_ref[0].astype(F32))
    row = lax.broadcasted_iota(jnp.int32, x.shape, 0)
    h1 = halo[CONV_HALO - 1:CONV_HALO, :]
    h2 = halo[CONV_HALO - 2:CONV_HALO - 1, :]
    x1 = jnp.where(row == 0, h1, pltpu.roll(x, 1, 0))
    x2 = jnp.where(row == 0, h2, jnp.where(row == 1, h1, pltpu.roll(x, 2, 0)))
    y = w_ref[0:1, :] * x2 + w_ref[1:2, :] * x1 + w_ref[2:3, :] * x
    o_ref[0] = (gb_ref[0].astype(F32) * y).astype(o_ref.dtype)


def _conv(proj, w):
    b, s, _ = proj.shape
    ts = min(512, s)
    c0 = 4 * HG_WIDTH // CONV_WIDTH
    main = lambda off: pl.BlockSpec((1, ts, CONV_WIDTH), lambda i, t: (i, t, c0 + off))
    halo = lambda off: pl.BlockSpec(
        (1, CONV_HALO, CONV_WIDTH),
        lambda i, t: (i, jnp.maximum(t * (ts // CONV_HALO) - 1, 0), c0 + off))
    return pl.pallas_call(
        _conv_kernel,
        grid=(b, s // ts),
        in_specs=[
            pl.BlockSpec((CONV_K, CONV_WIDTH), lambda i, t: (0, 0)),
            main(0), main(1), main(2), halo(1), halo(2),
        ],
        out_specs=pl.BlockSpec((1, ts, CONV_WIDTH), lambda i, t: (i, t, 0)),
        out_shape=jax.ShapeDtypeStruct((b, s, CONV_WIDTH), BF16),
        compiler_params=_params(("parallel", "arbitrary")),
        name="short_conv",
    )(w, proj, proj, proj, proj, proj)


def _attn_columns(w_in):
    split = SWA_Q + 2 * SWA_KV
    return jnp.concatenate([w_in[:, split:], w_in[:, :split]], axis=1)


def kernel(x, norm_g, ffn_w_in, ffn_w_out, attn_w_in, attn_sinks, attn_w_out, rec_w_in,
           hgrn_lb_logits, hgrn_norm_g, conv_w, rec_w_out, final_g):
    b, s, d = x.shape
    depth = norm_g.shape[0]
    m = b * s
    h = x.reshape(m, d)
    for layer in range(depth):
        h = _ffn(h, norm_g[layer, 0], ffn_w_in[layer, 0].astype(BF16), ffn_w_out[layer, 0].astype(BF16),
                 final_g, final_norm=False)
        if layer % 2 == 0:
            e = layer // 2
            proj = _proj_in(h, norm_g[layer, 1], _attn_columns(attn_w_in[e]).astype(BF16))[0]
            proj = proj.reshape(b, s, ATTN_IN)
            o_a = _swa(proj, attn_sinks[e])
            o_b = _sb(proj)
            mix = (o_a.reshape(m, SWA_Q), o_b.reshape(m, SB_W), attn_w_out[e].astype(BF16))
        else:
            o = layer // 2
            proj, f_logit = _proj_in(h, norm_g[layer, 1], rec_w_in[o].astype(BF16), (HG_WIDTH, 2 * HG_WIDTH))
            proj = proj.reshape(b, s, REC_IN)
            o_c = _hgrn(proj, f_logit.reshape(b, s, HG_WIDTH), hgrn_lb_logits, hgrn_norm_g[o], layer)
            o_d = _conv(proj, conv_w[o])
            mix = (o_c.reshape(m, HG_WIDTH), o_d.reshape(m, CONV_WIDTH), rec_w_out[o].astype(BF16))
        h = _ffn(h, norm_g[layer, 2], ffn_w_in[layer, 1].astype(BF16), ffn_w_out[layer, 1].astype(BF16),
                 final_g, final_norm=(layer == depth - 1), mix=mix)
    return h.reshape(b, s, d)
```

```python
import functools
import math

import jax
import jax.numpy as jnp
from jax import lax
from jax.experimental import pallas as pl
from jax.experimental.pallas import tpu as pltpu

F32 = jnp.float32
BF16 = jnp.bfloat16

D_MODEL = 1024
HEAD_DIM = 64
SWA_HEADS = 8
SWA_KV_HEADS = 2
SWA_GROUP = SWA_HEADS // SWA_KV_HEADS
WINDOW = 128
BLK = 128
SUBLANES = 8
SB_HEADS = 8
HG_WIDTH = 512
HG_DK = 128
HG_HEADS = HG_WIDTH // HG_DK
HG_CHUNK = 64
HG_SUB = 16
HG_UNROLL = 4
SWA_UNROLL = 4
CONV_WIDTH = 512
CONV_K = 3
D_FF = 2816
EPS = 1e-6

SWA_Q = SWA_HEADS * HEAD_DIM
SWA_KV = SWA_KV_HEADS * HEAD_DIM
SB_W = SB_HEADS * HEAD_DIM
ATTN_IN = SWA_Q + 2 * SWA_KV + 3 * SB_W
REC_IN = 4 * HG_WIDTH + 3 * CONV_WIDTH

VMEM_LIMIT_BYTES = 56 * 1024 * 1024


def _params(semantics):
    return pltpu.CompilerParams(dimension_semantics=semantics,
                                vmem_limit_bytes=VMEM_LIMIT_BYTES)


def _rms(x, g):
    return x * lax.rsqrt(jnp.mean(x * x, axis=-1, keepdims=True) + EPS) * g


def _dot(a, b):
    return jnp.dot(a, b, preferred_element_type=F32)


def _dot_nt(a, b):
    return lax.dot_general(a, b, (((1,), (1,)), ((), ())), preferred_element_type=F32)


def _dot_tn(a, b):
    return lax.dot_general(a, b, (((0,), (0,)), ((), ())), preferred_element_type=F32)


def _split_dot(x, m01):
    hi = x.astype(BF16)
    lo = (x - hi.astype(F32)).astype(BF16)
    return _dot(hi, m01) + _dot(lo, m01)


def _split_dot_left(m01, x):
    hi = x.astype(BF16)
    lo = (x - hi.astype(F32)).astype(BF16)
    return _dot(m01, hi) + _dot(m01, lo)


def _ffn_kernel(*refs, final_norm, with_mix):
    if with_mix:
        x_ref, a_ref, b_ref, wm_ref, g_ref, wi_ref, wo_ref, fg_ref, o_ref = refs
    else:
        x_ref, g_ref, wi_ref, wo_ref, fg_ref, o_ref = refs
    d_ff = wo_ref.shape[0]
    x = x_ref[...]
    if with_mix:
        ka = a_ref.shape[1]
        x = x + _dot(a_ref[...], wm_ref[:ka, :]) + _dot(b_ref[...], wm_ref[ka:, :])
    n = _rms(x, g_ref[...]).astype(BF16)
    gate = _dot(n, wi_ref[:, :d_ff])
    up = _dot(n, wi_ref[:, d_ff:])
    act = (gate * jax.nn.sigmoid(gate) * up).astype(BF16)
    h = x + 0.5 * _dot(act, wo_ref[...])
    if final_norm:
        h = _rms(h, fg_ref[...])
    o_ref[...] = h


FFN_TM = 512


def _ffn(h, g, w_in, w_out, final_g, *, final_norm, mix=None):
    m, d = h.shape
    d_ff = w_out.shape[0]
    tm = min(FFN_TM, m)
    rows = lambda width: pl.BlockSpec((tm, width), lambda i: (i, 0))
    resident = lambda shape: pl.BlockSpec(shape, lambda i: (0, 0), pipeline_mode=pl.Buffered(1))
    operands, in_specs = [h], [rows(d)]
    if mix is not None:
        a, b, w_mix = mix
        operands += [a, b, w_mix]
        in_specs += [rows(a.shape[1]), rows(b.shape[1]), resident(w_mix.shape)]
    operands += [g.reshape(1, d), w_in, w_out, final_g.reshape(1, d)]
    in_specs += [resident((1, d)), resident((d, 2 * d_ff)), resident((d_ff, d)), resident((1, d))]
    return pl.pallas_call(
        functools.partial(_ffn_kernel, final_norm=final_norm, with_mix=mix is not None),
        grid=(m // tm,),
        in_specs=in_specs,
        out_specs=rows(d),
        out_shape=jax.ShapeDtypeStruct((m, d), F32),
        compiler_params=_params(("parallel",)),
        name="ffn",
    )(*operands)


def _proj_in_kernel(x_ref, g_ref, w_ref, o_ref, *wide_refs, wide_cols):
    y = _dot(_rms(x_ref[...], g_ref[...]).astype(BF16), w_ref[...])
    o_ref[...] = y.astype(o_ref.dtype)
    if wide_cols is not None:
        wide_refs[0][...] = y[:, wide_cols[0]:wide_cols[1]]


PROJ_TM = 512


def _proj_in(h, g, w, wide_cols=None):
    m, d = h.shape
    n_out = w.shape[1]
    tm = min(PROJ_TM, m)
    resident = lambda shape: pl.BlockSpec(shape, lambda i: (0, 0), pipeline_mode=pl.Buffered(1))
    out_specs = [pl.BlockSpec((tm, n_out), lambda i: (i, 0))]
    out_shape = [jax.ShapeDtypeStruct((m, n_out), BF16)]
    if wide_cols is not None:
        out_specs.append(pl.BlockSpec((tm, wide_cols[1] - wide_cols[0]), lambda i: (i, 0)))
        out_shape.append(jax.ShapeDtypeStruct((m, wide_cols[1] - wide_cols[0]), F32))
    return pl.pallas_call(
        functools.partial(_proj_in_kernel, wide_cols=wide_cols),
        grid=(m // tm,),
        in_specs=[
            pl.BlockSpec((tm, d), lambda i: (i, 0)),
            resident((1, d)),
            resident((d, n_out)),
        ],
        out_specs=out_specs,
        out_shape=out_shape,
        compiler_params=_params(("parallel",)),
        name="proj_in",
    )(h, g.reshape(1, d), w)


def _swa_kernel(sink_ref, q_ref, k_ref, v_ref, o_ref, k_swapped, v_swapped):
    nb = q_ref.shape[1] // BLK
    half = HEAD_DIM
    k_swapped[...] = pltpu.roll(k_ref[0].astype(F32), half, 1).astype(BF16)
    v_swapped[...] = pltpu.roll(v_ref[0].astype(F32), half, 1).astype(BF16)
    qi = lax.broadcasted_iota(jnp.int32, (BLK, 2 * BLK), 0)
    kj = lax.broadcasted_iota(jnp.int32, (BLK, 2 * BLK), 1)
    dist = qi + BLK - kj
    in_window = (dist >= 0) & (dist < WINDOW)
    dist_f = dist.astype(F32)
    lane = lax.broadcasted_iota(jnp.int32, (BLK, BLK), 1)
    scale = HEAD_DIM ** -0.5

    def q_block(n, _):
        cur = pl.ds(pl.multiple_of(n * BLK, BLK), BLK)
        prev = pl.ds(pl.multiple_of(jnp.maximum(n - 1, 0) * BLK, BLK), BLK)
        valid = in_window & (n * BLK - BLK + kj >= 0)
        penalty = jnp.where(valid, dist_f, jnp.inf)
        band = lambda ref: jnp.concatenate([ref[prev, :], ref[cur, :]], axis=0)
        k_by_half = (band(k_ref.at[0]), band(k_swapped))
        v_by_half = (band(v_ref.at[0]), band(v_swapped))
        heads = range(SWA_HEADS)
        kv_half = [(h // SWA_GROUP) ^ (h % 2) for h in heads]
        scores = []
        for h in heads:
            q_col = q_ref[0, cur, (h // 2) * BLK:(h // 2 + 1) * BLK].astype(F32) * scale
            in_half = (lane >= (h % 2) * half) & (lane < (h % 2 + 1) * half)
            qm = jnp.where(in_half, q_col, 0.0).astype(BF16)
            slope = 2.0 ** (-8.0 * (h + 1) / SWA_HEADS)
            scores.append(_dot_nt(qm, k_by_half[kv_half[h]]) - slope * penalty)
        row_max = [jnp.maximum(jnp.max(scores[h], axis=-1, keepdims=True), sink_ref[h]) for h in heads]
        probs = [jnp.exp(scores[h] - row_max[h]) for h in heads]
        num = [_dot(probs[h].astype(BF16), v_by_half[kv_half[h]]) for h in heads]
        den = [jnp.sum(probs[h], axis=-1, keepdims=True) + jnp.exp(sink_ref[h] - row_max[h]) for h in heads]
        out = [num[h] / den[h] for h in heads]
        for col in range(SWA_Q // BLK):
            o_ref[0, cur, col * BLK:(col + 1) * BLK] = jnp.where(
                lane < half, out[2 * col], out[2 * col + 1]).astype(o_ref.dtype)
        return 0

    lax.fori_loop(0, nb, q_block, 0, unroll=SWA_UNROLL)


def _swa(proj, sinks):
    b, s, _ = proj.shape
    assert SWA_KV == BLK and SWA_KV_HEADS == 2
    q_col = 3 * SB_W // SWA_Q
    k_col = (3 * SB_W + SWA_Q) // SWA_KV
    v_col = k_col + 1
    return pl.pallas_call(
        _swa_kernel,
        grid=(b,),
        in_specs=[
            pl.BlockSpec(memory_space=pltpu.SMEM),
            pl.BlockSpec((1, s, SWA_Q), lambda i: (i, 0, q_col)),
            pl.BlockSpec((1, s, SWA_KV), lambda i: (i, 0, k_col)),
            pl.BlockSpec((1, s, SWA_KV), lambda i: (i, 0, v_col)),
        ],
        out_specs=pl.BlockSpec((1, s, SWA_Q), lambda i: (i, 0, 0)),
        out_shape=jax.ShapeDtypeStruct((b, s, SWA_Q), BF16),
        scratch_shapes=[pltpu.VMEM((s, SWA_KV), BF16), pltpu.VMEM((s, SWA_KV), BF16)],
        compiler_params=_params(("parallel",)),
        name="swa",
    )(sinks, proj, proj, proj)


SB_TQ = 256
SB_KPI = 2
LOG2E = math.log2(math.e)
SB_DEAD = -256.0


SB_PAIRS = 4


def _sb_kernel(q_ref, k_ref, v_ref, o_ref, kk_scr, vv_scr, carry_scr, acc_scr):
    s = q_ref.shape[1]
    nt = s // SB_TQ
    nb = s // BLK
    kpt = SB_TQ // BLK
    lane = lax.broadcasted_iota(jnp.int32, (BLK, BLK), 1)
    key_pos = lax.broadcasted_iota(jnp.int32, (SB_TQ, 2 * BLK), 1) & (BLK - 1)
    q_pos = lax.broadcasted_iota(jnp.int32, (SB_TQ, 2 * BLK), 0)
    jj = lax.broadcasted_iota(jnp.int32, (2 * BLK, 2 * BLK), 0) & (BLK - 1)
    ss = lax.broadcasted_iota(jnp.int32, (2 * BLK, 2 * BLK), 1)
    suffix_mat = jnp.where((jj > ss) | (ss >= BLK), 1.0, 0.0).astype(BF16)
    scale = HEAD_DIM ** -0.5

    def prep(j, _):
        rows = pl.ds(pl.multiple_of(j * BLK, BLK), BLK)
        for p in range(SB_PAIRS):
            kb = k_ref[0, rows, p * BLK:(p + 1) * BLK]
            vb = v_ref[0, rows, p * BLK:(p + 1) * BLK]
            kk_scr[p, j] = jnp.concatenate([jnp.where(lane < HEAD_DIM, kb, 0.0),
                                            jnp.where(lane >= HEAD_DIM, kb, 0.0)], axis=0).astype(BF16)
            vv_scr[p, j] = jnp.concatenate([jnp.where(lane < HEAD_DIM, vb, 0.0),
                                            jnp.where(lane >= HEAD_DIM, vb, 0.0)], axis=0).astype(BF16)
        return 0

    lax.fori_loop(0, nb, prep, 0)

    def key_blocks(qn, t, js, masked):
        first_row = [(len(js) - 1 - d) * BLK if masked else 0 for d in range(len(js))]
        chains = [(p, d) for p in range(SB_PAIRS) for d in range(len(js))]
        causal = [((js[d] * BLK + key_pos) < (t * SB_TQ + q_pos))[first_row[d]:] if masked else None
                  for d in range(len(js))]
        nz = {(p, d): _dot_nt(qn[p][first_row[d]:], kk_scr[p, js[d]]) for p, d in chains}
        log_beta, lhs = {}, {}
        for c in chains:
            log_keep = jnp.minimum(nz[c], 0.0) - jnp.log2(1.0 + jnp.exp2(-jnp.abs(nz[c])))
            log_beta[c] = log_keep - nz[c]
            if masked:
                log_keep = jnp.where(causal[c[1]], log_keep, 0.0)
            hi = log_keep.astype(BF16)
            lo = (log_keep - hi.astype(F32)).astype(BF16)
            lhs[c] = [jnp.concatenate([hi[:, h * BLK:(h + 1) * BLK], lo[:, h * BLK:(h + 1) * BLK]], axis=1)
                      for h in range(2)]
        cs = {c: [_dot(lhs[c][h], suffix_mat) for h in range(2)] for c in chains}
        w = {}
        for p in range(SB_PAIRS):
            carry = carry_scr[p]
            for d in range(len(js)):
                c, r0 = (p, d), first_row[d]
                between = jnp.concatenate([cs[c][0][:, :BLK], cs[c][1][:, :BLK]], axis=1)
                wc = jnp.exp2(log_beta[c] + between + carry[r0:])
                if masked:
                    wc = jnp.where(causal[d], wc, 0.0)
                w[c] = wc.astype(BF16)
                below = carry[r0:] + jnp.concatenate([cs[c][0][:, BLK:], cs[c][1][:, BLK:]], axis=1)
                carry = below if r0 == 0 else jnp.concatenate([carry[:r0], below], axis=0)
            carry_scr[p] = carry
        for p in range(SB_PAIRS):
            acc = acc_scr[p]
            for d, j in enumerate(js):
                r0 = first_row[d]
                below = acc[r0:] + _dot(w[(p, d)], vv_scr[p, j])
                acc = below if r0 == 0 else jnp.concatenate([acc[:r0], below], axis=0)
            acc_scr[p] = acc

    def q_tile(t, _):
        rows = pl.ds(pl.multiple_of(t * SB_TQ, SB_TQ), SB_TQ)
        qn = [(q_ref[0, rows, p * BLK:(p + 1) * BLK].astype(F32) * (-scale * LOG2E)).astype(BF16)
              for p in range(SB_PAIRS)]
        carry_scr[...] = jnp.zeros_like(carry_scr)
        acc_scr[...] = jnp.zeros_like(acc_scr)
        key_blocks(qn, t, [kpt * t + kpt - 1 - d for d in range(kpt)], True)

        def live(state):
            return (state[0] < kpt * t // SB_KPI) & (state[1] > SB_DEAD)

        def kv_step(state):
            i = state[0]
            key_blocks(qn, t, [kpt * t - 1 - i * SB_KPI - d for d in range(SB_KPI)], False)
            return i + 1, jnp.max(carry_scr[...])

        lax.while_loop(live, kv_step, (jnp.int32(0), jnp.float32(0.0)))
        for p in range(SB_PAIRS):
            o_ref[0, rows, p * BLK:(p + 1) * BLK] = acc_scr[p].astype(o_ref.dtype)
        return 0

    lax.fori_loop(0, nt, q_tile, 0)


def _sb(proj):
    b, s, _ = proj.shape
    assert (SB_TQ // BLK) % SB_KPI == 0
    width = SB_PAIRS * BLK
    q_col = 0
    k_col = SB_W // width
    v_col = 2 * SB_W // width
    return pl.pallas_call(
        _sb_kernel,
        grid=(b, SB_W // width),
        in_specs=[
            pl.BlockSpec((1, s, width), lambda i, p: (i, 0, q_col + p)),
            pl.BlockSpec((1, s, width), lambda i, p: (i, 0, k_col + p)),
            pl.BlockSpec((1, s, width), lambda i, p: (i, 0, v_col + p)),
        ],
        out_specs=pl.BlockSpec((1, s, width), lambda i, p: (i, 0, p)),
        out_shape=jax.ShapeDtypeStruct((b, s, SB_W), BF16),
        scratch_shapes=[pltpu.VMEM((SB_PAIRS, s // BLK, 2 * BLK, BLK), BF16),
                        pltpu.VMEM((SB_PAIRS, s // BLK, 2 * BLK, BLK), BF16),
                        pltpu.VMEM((SB_PAIRS, SB_TQ, 2 * BLK), F32), pltpu.VMEM((SB_PAIRS, SB_TQ, BLK), F32)],
        compiler_params=_params(("parallel", "parallel")),
        name="stick_breaking",
    )(proj, proj, proj)


def _hgrn_kernel(layer, lbl_ref, ng_ref, q_ref, f_ref, i_ref, gate_ref, o_ref, state_scr):
    nc = q_ref.shape[1] // HG_CHUNK
    logits = lbl_ref[...]
    e = jnp.exp(logits - jnp.max(logits, axis=0, keepdims=True))
    sm = e / jnp.sum(e, axis=0, keepdims=True)
    lb = jnp.sum(sm[1:layer + 1], axis=0, keepdims=True)
    ti = lax.broadcasted_iota(jnp.int32, (HG_CHUNK, HG_CHUNK), 0)
    si = lax.broadcasted_iota(jnp.int32, (HG_CHUNK, HG_CHUNK), 1)
    cumsum_mat = jnp.where(si <= ti, 1.0, 0.0).astype(BF16)
    row8 = lax.broadcasted_iota(jnp.int32, (SUBLANES, 1), 0)
    n_sub = HG_CHUNK // HG_SUB
    state_scr[...] = jnp.zeros_like(state_scr)

    def pair_rows(acc, qt, gt, ks, gs, vs, s):
        decay = jnp.exp2(gt - gs[s:s + 1, :])
        col = jnp.sum(qt * (ks[s:s + 1, :] * decay), axis=-1, keepdims=True)
        col = jnp.where(row8 >= s, col, 0.0)
        return acc + col * vs[s:s + 1, :]

    def chunk(c, _):
        rows = pl.ds(pl.multiple_of(c * HG_CHUNK, HG_CHUNK), HG_CHUNK)
        z = f_ref[0, rows, :]
        qx = q_ref[0, rows, :].astype(F32)
        v_all = i_ref[0, rows, :].astype(F32)
        gate = gate_ref[0, rows, :].astype(F32)
        sig = jax.nn.sigmoid(z)
        q_all = qx * jax.nn.sigmoid(qx)
        k_all = (1.0 - lb) * (1.0 - sig)
        log2_f = jnp.log2(lb + (1.0 - lb) * sig)
        g_all = _split_dot_left(cumsum_mat, log2_f)
        g_last_all = g_all[HG_CHUNK - 1:HG_CHUNK, :]
        qg_all = (q_all * jnp.exp2(g_all)).astype(BF16)
        kd_all = (k_all * jnp.exp2(g_last_all - g_all)).astype(BF16)
        gated = gate * jax.nn.sigmoid(gate)
        kt_all, qt_all = [], []
        for j in range(n_sub - 1):
            lo, hi = j * HG_SUB, (j + 1) * HG_SUB
            r = g_all[hi - 1:hi, :]
            kt_all.append((k_all[lo:hi] * jnp.exp2(r - g_all[lo:hi])).astype(BF16))
            qt_all.append((q_all[hi:] * jnp.exp2(g_all[hi:] - r)).astype(BF16))
        q_low, k_up, v_up = [], [], []
        for i in range(n_sub):
            lo, mid, hi = i * HG_SUB, i * HG_SUB + SUBLANES, (i + 1) * HG_SUB
            r = g_all[mid - 1:mid, :]
            q_low.append(q_all[mid:hi] * jnp.exp2(g_all[mid:hi] - r))
            k_up.append(k_all[lo:mid] * jnp.exp2(r - g_all[lo:mid]))
            v_up.append(v_all[lo:mid])
        q_low = jnp.concatenate(q_low, axis=0).astype(BF16)
        k_up = jnp.concatenate(k_up, axis=0).astype(BF16)
        v_up = jnp.concatenate(v_up, axis=0).astype(BF16)
        same_sub = (lax.broadcasted_iota(jnp.int32, (n_sub * SUBLANES, n_sub * SUBLANES), 0) // SUBLANES
                    == lax.broadcasted_iota(jnp.int32, (n_sub * SUBLANES, n_sub * SUBLANES), 1) // SUBLANES)
        heads = range(HG_HEADS)
        cols = [slice(h * HG_DK, (h + 1) * HG_DK) for h in heads]
        v_bf = v_all.astype(BF16)
        state = [state_scr[h] for h in heads]
        o_inter = [_dot_nt(qg_all[:, cols[h]], state[h].astype(BF16)) for h in heads]
        sc = [[_dot_nt(qt_all[j][:, cols[h]], kt_all[j][:, cols[h]]).astype(BF16)
               for j in range(n_sub - 1)] for h in heads]
        contrib = [[_dot(sc[h][j], v_bf[j * HG_SUB:(j + 1) * HG_SUB, cols[h]])
                    for j in range(n_sub - 1)] for h in heads]
        new_state = [state[h] * jnp.exp2(g_last_all[:, cols[h]]) + _dot_tn(v_bf[:, cols[h]], kd_all[:, cols[h]])
                     for h in heads]
        sc_low = [jnp.where(same_sub, _dot_nt(q_low[:, cols[h]], k_up[:, cols[h]]), 0.0).astype(BF16) for h in heads]
        contrib_low = [_dot(sc_low[h], v_up[:, cols[h]]) for h in heads]
        tiles = [[None] * (HG_CHUNK // SUBLANES) for _ in heads]
        for h in heads:
            for i in range(n_sub):
                part = jnp.zeros((HG_SUB, HG_DK), F32)
                for j in range(i):
                    part = part + contrib[h][j][(i - j - 1) * HG_SUB:(i - j) * HG_SUB]
                tiles[h][2 * i] = part[:SUBLANES]
                tiles[h][2 * i + 1] = part[SUBLANES:] + contrib_low[h][i * SUBLANES:(i + 1) * SUBLANES]
        for i in range(2 * n_sub):
            a = slice(i * SUBLANES, (i + 1) * SUBLANES)
            for s in range(SUBLANES):
                for h in heads:
                    g, q, k, v = g_all[:, cols[h]], q_all[:, cols[h]], k_all[:, cols[h]], v_all[:, cols[h]]
                    tiles[h][i] = pair_rows(tiles[h][i], q[a], g[a], k[a], g[a], v[a], s)
        for h in heads:
            o = o_inter[h] + jnp.concatenate(tiles[h], axis=0)
            state_scr[h] = new_state[h]
            o_ref[0, rows, cols[h]] = (_rms(o, ng_ref[...]) * gated[:, cols[h]]).astype(o_ref.dtype)
        return 0

    lax.fori_loop(0, nc, chunk, 0, unroll=HG_UNROLL)


def _hgrn(proj, f_logit, lb_logits, norm_g, layer):
    b, s, _ = proj.shape
    depth = lb_logits.shape[0]
    assert HG_SUB == 2 * SUBLANES
    blk = lambda off: pl.BlockSpec((1, s, HG_WIDTH), lambda i: (i, 0, off))
    return pl.pallas_call(
        functools.partial(_hgrn_kernel, layer),
        grid=(b,),
        in_specs=[
            pl.BlockSpec((depth, HG_WIDTH), lambda i: (0, 0)),
            pl.BlockSpec((1, HG_DK), lambda i: (0, 0)),
            blk(0), blk(0), blk(2), blk(3),
        ],
        out_specs=pl.BlockSpec((1, s, HG_WIDTH), lambda i: (i, 0, 0)),
        out_shape=jax.ShapeDtypeStruct((b, s, HG_WIDTH), BF16),
        scratch_shapes=[pltpu.VMEM((HG_HEADS, HG_DK, HG_DK), F32)],
        compiler_params=_params(("parallel",)),
        name="hgrn2",
    )(lb_logits, norm_g.reshape(1, HG_DK), proj, f_logit, proj, proj)


CONV_HALO = 16


def _conv_kernel(w_ref, gb_ref, gc_ref, u_ref, gch_ref, uh_ref, o_ref):
    first = pl.program_id(1) == 0
    x = gc_ref[0].astype(F32) * u_ref[0].astype(F32)
    halo = jnp.where(first, 0.0, gch_ref[0].astype(F32) * uh_ref[0].astype(F32))
    row = lax.broadcasted_iota(jnp.int32, x.shape, 0)
    h1 = halo[CONV_HALO - 1:CONV_HALO, :]
    h2 = halo[CONV_HALO - 2:CONV_HALO - 1, :]
    x1 = jnp.where(row == 0, h1, pltpu.roll(x, 1, 0))
    x2 = jnp.where(row == 0, h2, jnp.where(row == 1, h1, pltpu.roll(x, 2, 0)))
    y = w_ref[0:1, :] * x2 + w_ref[1:2, :] * x1 + w_ref[2:3, :] * x
    o_ref[0] = (gb_ref[0].astype(F32) * y).astype(o_ref.dtype)


def _conv(proj, w):
    b, s, _ = proj.shape
    ts = min(512, s)
    c0 = 4 * HG_WIDTH // CONV_WIDTH
    main = lambda off: pl.BlockSpec((1, ts, CONV_WIDTH), lambda i, t: (i, t, c0 + off))
    halo = lambda off: pl.BlockSpec(
        (1, CONV_HALO, CONV_WIDTH),
        lambda i, t: (i, jnp.maximum(t * (ts // CONV_HALO) - 1, 0), c0 + off))
    return pl.pallas_call(
        _conv_kernel,
        grid=(b, s // ts),
        in_specs=[
            pl.BlockSpec((CONV_K, CONV_WIDTH), lambda i, t: (0, 0)),
            main(0), main(1), main(2), halo(1), halo(2),
        ],
        out_specs=pl.BlockSpec((1, ts, CONV_WIDTH), lambda i, t: (i, t, 0)),
        out_shape=jax.ShapeDtypeStruct((b, s, CONV_WIDTH), BF16),
        compiler_params=_params(("parallel", "arbitrary")),
        name="short_conv",
    )(w, proj, proj, proj, proj, proj)


def _attn_columns(w_in):
    split = SWA_Q + 2 * SWA_KV
    return jnp.concatenate([w_in[:, split:], w_in[:, :split]], axis=1)


def kernel(x, norm_g, ffn_w_in, ffn_w_out, attn_w_in, attn_sinks, attn_w_out, rec_w_in,
           hgrn_lb_logits, hgrn_norm_g, conv_w, rec_w_out, final_g):
    b, s, d = x.shape
    depth = norm_g.shape[0]
    m = b * s
    h = x.reshape(m, d)
    for layer in range(depth):
        h = _ffn(h, norm_g[layer, 0], ffn_w_in[layer, 0].astype(BF16), ffn_w_out[layer, 0].astype(BF16),
                 final_g, final_norm=False)
        if layer % 2 == 0:
            e = layer // 2
            proj = _proj_in(h, norm_g[layer, 1], _attn_columns(attn_w_in[e]).astype(BF16))[0]
            proj = proj.reshape(b, s, ATTN_IN)
            o_a = _swa(proj, attn_sinks[e])
            o_b = _sb(proj)
            mix = (o_a.reshape(m, SWA_Q), o_b.reshape(m, SB_W), attn_w_out[e].astype(BF16))
        else:
            o = layer // 2
            proj, f_logit = _proj_in(h, norm_g[layer, 1], rec_w_in[o].astype(BF16), (HG_WIDTH, 2 * HG_WIDTH))
            proj = proj.reshape(b, s, REC_IN)
            o_c = _hgrn(proj, f_logit.reshape(b, s, HG_WIDTH), hgrn_lb_logits, hgrn_norm_g[o], layer)
            o_d = _conv(proj, conv_w[o])
            mix = (o_c.reshape(m, HG_WIDTH), o_d.reshape(m, CONV_WIDTH), rec_w_out[o].astype(BF16))
        h = _ffn(h, norm_g[layer, 2], ffn_w_in[layer, 1].astype(BF16), ffn_w_out[layer, 1].astype(BF16),
                 final_g, final_norm=(layer == depth - 1), mix=mix)
    return h.reshape(b, s, d)
```

```python
import functools
import math

import jax
import jax.numpy as jnp
from jax import lax
from jax.experimental import pallas as pl
from jax.experimental.pallas import tpu as pltpu

F32 = jnp.float32
BF16 = jnp.bfloat16

D_MODEL = 1024
HEAD_DIM = 64
SWA_HEADS = 8
SWA_KV_HEADS = 2
SWA_GROUP = SWA_HEADS // SWA_KV_HEADS
WINDOW = 128
BLK = 128
SUBLANES = 8
SB_HEADS = 8
HG_WIDTH = 512
HG_DK = 128
HG_HEADS = HG_WIDTH // HG_DK
HG_CHUNK = 64
HG_SUB = 16
HG_UNROLL = 4
SWA_UNROLL = 4
CONV_WIDTH = 512
CONV_K = 3
D_FF = 2816
EPS = 1e-6

SWA_Q = SWA_HEADS * HEAD_DIM
SWA_KV = SWA_KV_HEADS * HEAD_DIM
SB_W = SB_HEADS * HEAD_DIM
ATTN_IN = SWA_Q + 2 * SWA_KV + 3 * SB_W
REC_IN = 4 * HG_WIDTH + 3 * CONV_WIDTH

VMEM_LIMIT_BYTES = 56 * 1024 * 1024


def _params(semantics):
    return pltpu.CompilerParams(dimension_semantics=semantics,
                                vmem_limit_bytes=VMEM_LIMIT_BYTES)


def _rms(x, g):
    return x * lax.rsqrt(jnp.mean(x * x, axis=-1, keepdims=True) + EPS) * g


def _dot(a, b):
    return jnp.dot(a, b, preferred_element_type=F32)


def _dot_nt(a, b):
    return lax.dot_general(a, b, (((1,), (1,)), ((), ())), preferred_element_type=F32)


def _dot_tn(a, b):
    return lax.dot_general(a, b, (((0,), (0,)), ((), ())), preferred_element_type=F32)


def _split_dot(x, m01):
    hi = x.astype(BF16)
    lo = (x - hi.astype(F32)).astype(BF16)
    return _dot(hi, m01) + _dot(lo, m01)


def _split_dot_left(m01, x):
    hi = x.astype(BF16)
    lo = (x - hi.astype(F32)).astype(BF16)
    return _dot(m01, hi) + _dot(m01, lo)


def _ffn_kernel(*refs, final_norm, with_mix):
    if with_mix:
        x_ref, a_ref, b_ref, wm_ref, g_ref, wi_ref, wo_ref, fg_ref, o_ref = refs
    else:
        x_ref, g_ref, wi_ref, wo_ref, fg_ref, o_ref = refs
    d_ff = wo_ref.shape[0]
    x = x_ref[...]
    if with_mix:
        ka = a_ref.shape[1]
        x = x + _dot(a_ref[...], wm_ref[:ka, :]) + _dot(b_ref[...], wm_ref[ka:, :])
    n = _rms(x, g_ref[...]).astype(BF16)
    gate = _dot(n, wi_ref[:, :d_ff])
    up = _dot(n, wi_ref[:, d_ff:])
    act = (gate * jax.nn.sigmoid(gate) * up).astype(BF16)
    h = x + 0.5 * _dot(act, wo_ref[...])
    if final_norm:
        h = _rms(h, fg_ref[...])
    o_ref[...] = h


FFN_TM = 512


def _ffn(h, g, w_in, w_out, final_g, *, final_norm, mix=None):
    m, d = h.shape
    d_ff = w_out.shape[0]
    tm = min(FFN_TM, m)
    rows = lambda width: pl.BlockSpec((tm, width), lambda i: (i, 0))
    resident = lambda shape: pl.BlockSpec(shape, lambda i: (0, 0), pipeline_mode=pl.Buffered(1))
    operands, in_specs = [h], [rows(d)]
    if mix is not None:
        a, b, w_mix = mix
        operands += [a, b, w_mix]
        in_specs += [rows(a.shape[1]), rows(b.shape[1]), resident(w_mix.shape)]
    operands += [g.reshape(1, d), w_in, w_out, final_g.reshape(1, d)]
    in_specs += [resident((1, d)), resident((d, 2 * d_ff)), resident((d_ff, d)), resident((1, d))]
    return pl.pallas_call(
        functools.partial(_ffn_kernel, final_norm=final_norm, with_mix=mix is not None),
        grid=(m // tm,),
        in_specs=in_specs,
        out_specs=rows(d),
        out_shape=jax.ShapeDtypeStruct((m, d), F32),
        compiler_params=_params(("parallel",)),
        name="ffn",
    )(*operands)


def _proj_in_kernel(x_ref, g_ref, w_ref, o_ref):
    o_ref[...] = _dot(_rms(x_ref[...], g_ref[...]).astype(BF16), w_ref[...]).astype(o_ref.dtype)


PROJ_TM = 512


def _proj_in(h, g, w):
    m, d = h.shape
    n_out = w.shape[1]
    tm = min(PROJ_TM, m)
    resident = lambda shape: pl.BlockSpec(shape, lambda i: (0, 0), pipeline_mode=pl.Buffered(1))
    return pl.pallas_call(
        _proj_in_kernel,
        grid=(m // tm,),
        in_specs=[pl.BlockSpec((tm, d), lambda i: (i, 0)), resident((1, d)), resident((d, n_out))],
        out_specs=pl.BlockSpec((tm, n_out), lambda i: (i, 0)),
        out_shape=jax.ShapeDtypeStruct((m, n_out), BF16),
        compiler_params=_params(("parallel",)),
        name="proj_in",
    )(h, g.reshape(1, d), w)


def _proj_rec_kernel(x_ref, g_ref, w_ref, cw_ref, o_ref, f_ref, od_ref, tail_scr, *, seq_tiles):
    y = _dot(_rms(x_ref[...], g_ref[...]).astype(BF16), w_ref[...])
    hg = 4 * HG_WIDTH
    o_ref[...] = y[:, :hg].astype(o_ref.dtype)
    f_ref[...] = y[:, HG_WIDTH:2 * HG_WIDTH]
    gate_b = y[:, hg:hg + CONV_WIDTH]
    x = y[:, hg + CONV_WIDTH:hg + 2 * CONV_WIDTH] * y[:, hg + 2 * CONV_WIDTH:]
    first = pl.program_id(0) % seq_tiles == 0
    tail = jnp.where(first, 0.0, tail_scr[...])
    row = lax.broadcasted_iota(jnp.int32, x.shape, 0)
    t1 = tail[SUBLANES - 1:SUBLANES, :]
    t2 = tail[SUBLANES - 2:SUBLANES - 1, :]
    x1 = jnp.where(row == 0, t1, pltpu.roll(x, 1, 0))
    x2 = jnp.where(row == 0, t2, jnp.where(row == 1, t1, pltpu.roll(x, 2, 0)))
    conv = cw_ref[0:1, :] * x2 + cw_ref[1:2, :] * x1 + cw_ref[2:3, :] * x
    od_ref[...] = (gate_b * conv).astype(od_ref.dtype)
    tail_scr[...] = x[x.shape[0] - SUBLANES:, :]


def _proj_rec(h, g, w, conv_w, seq_len):
    m, d = h.shape
    tm = min(PROJ_TM, seq_len)
    assert CONV_K - 1 <= SUBLANES and seq_len % tm == 0
    resident = lambda shape: pl.BlockSpec(shape, lambda i: (0, 0), pipeline_mode=pl.Buffered(1))
    rows = lambda width: pl.BlockSpec((tm, width), lambda i: (i, 0))
    return pl.pallas_call(
        functools.partial(_proj_rec_kernel, seq_tiles=seq_len // tm),
        grid=(m // tm,),
        in_specs=[rows(d), resident((1, d)), resident((d, REC_IN)), resident((CONV_K, CONV_WIDTH))],
        out_specs=[rows(4 * HG_WIDTH), rows(HG_WIDTH), rows(CONV_WIDTH)],
        out_shape=[jax.ShapeDtypeStruct((m, 4 * HG_WIDTH), BF16), jax.ShapeDtypeStruct((m, HG_WIDTH), F32),
                   jax.ShapeDtypeStruct((m, CONV_WIDTH), BF16)],
        scratch_shapes=[pltpu.VMEM((SUBLANES, CONV_WIDTH), F32)],
        compiler_params=_params(("arbitrary",)),
        name="proj_rec",
    )(h, g.reshape(1, d), w, conv_w)


def _swa_kernel(sink_ref, q_ref, k_ref, v_ref, o_ref, k_swapped, v_swapped):
    nb = q_ref.shape[1] // BLK
    half = HEAD_DIM
    k_swapped[...] = pltpu.roll(k_ref[0].astype(F32), half, 1).astype(BF16)
    v_swapped[...] = pltpu.roll(v_ref[0].astype(F32), half, 1).astype(BF16)
    qi = lax.broadcasted_iota(jnp.int32, (BLK, 2 * BLK), 0)
    kj = lax.broadcasted_iota(jnp.int32, (BLK, 2 * BLK), 1)
    dist = qi + BLK - kj
    in_window = (dist >= 0) & (dist < WINDOW)
    dist_f = dist.astype(F32)
    lane = lax.broadcasted_iota(jnp.int32, (BLK, BLK), 1)
    scale = HEAD_DIM ** -0.5

    def q_block(n, _):
        cur = pl.ds(pl.multiple_of(n * BLK, BLK), BLK)
        prev = pl.ds(pl.multiple_of(jnp.maximum(n - 1, 0) * BLK, BLK), BLK)
        valid = in_window & (n * BLK - BLK + kj >= 0)
        penalty = jnp.where(valid, dist_f, jnp.inf)
        band = lambda ref: jnp.concatenate([ref[prev, :], ref[cur, :]], axis=0)
        k_by_half = (band(k_ref.at[0]), band(k_swapped))
        v_by_half = (band(v_ref.at[0]), band(v_swapped))
        heads = range(SWA_HEADS)
        kv_half = [(h // SWA_GROUP) ^ (h % 2) for h in heads]
        scores = []
        for h in heads:
            q_col = q_ref[0, cur, (h // 2) * BLK:(h // 2 + 1) * BLK].astype(F32) * scale
            in_half = (lane >= (h % 2) * half) & (lane < (h % 2 + 1) * half)
            qm = jnp.where(in_half, q_col, 0.0).astype(BF16)
            slope = 2.0 ** (-8.0 * (h + 1) / SWA_HEADS)
            scores.append(_dot_nt(qm, k_by_half[kv_half[h]]) - slope * penalty)
        row_max = [jnp.maximum(jnp.max(scores[h], axis=-1, keepdims=True), sink_ref[h]) for h in heads]
        probs = [jnp.exp(scores[h] - row_max[h]) for h in heads]
        num = [_dot(probs[h].astype(BF16), v_by_half[kv_half[h]]) for h in heads]
        den = [jnp.sum(probs[h], axis=-1, keepdims=True) + jnp.exp(sink_ref[h] - row_max[h]) for h in heads]
        out = [num[h] / den[h] for h in heads]
        for col in range(SWA_Q // BLK):
            o_ref[0, cur, col * BLK:(col + 1) * BLK] = jnp.where(
                lane < half, out[2 * col], out[2 * col + 1]).astype(o_ref.dtype)
        return 0

    lax.fori_loop(0, nb, q_block, 0, unroll=SWA_UNROLL)


def _swa(proj, sinks):
    b, s, _ = proj.shape
    assert SWA_KV == BLK and SWA_KV_HEADS == 2
    q_col = 3 * SB_W // SWA_Q
    k_col = (3 * SB_W + SWA_Q) // SWA_KV
    v_col = k_col + 1
    return pl.pallas_call(
        _swa_kernel,
        grid=(b,),
        in_specs=[
            pl.BlockSpec(memory_space=pltpu.SMEM),
            pl.BlockSpec((1, s, SWA_Q), lambda i: (i, 0, q_col)),
            pl.BlockSpec((1, s, SWA_KV), lambda i: (i, 0, k_col)),
            pl.BlockSpec((1, s, SWA_KV), lambda i: (i, 0, v_col)),
        ],
        out_specs=pl.BlockSpec((1, s, SWA_Q), lambda i: (i, 0, 0)),
        out_shape=jax.ShapeDtypeStruct((b, s, SWA_Q), BF16),
        scratch_shapes=[pltpu.VMEM((s, SWA_KV), BF16), pltpu.VMEM((s, SWA_KV), BF16)],
        compiler_params=_params(("parallel",)),
        name="swa",
    )(sinks, proj, proj, proj)


SB_TQ = 256
SB_KPI = 2
LOG2E = math.log2(math.e)
SB_DEAD = -256.0


SB_PAIRS = 4


def _sb_kernel(q_ref, k_ref, v_ref, o_ref, kk_scr, vv_scr, carry_scr, acc_scr):
    s = q_ref.shape[1]
    nt = s // SB_TQ
    nb = s // BLK
    kpt = SB_TQ // BLK
    lane = lax.broadcasted_iota(jnp.int32, (BLK, BLK), 1)
    key_pos = lax.broadcasted_iota(jnp.int32, (SB_TQ, 2 * BLK), 1) & (BLK - 1)
    q_pos = lax.broadcasted_iota(jnp.int32, (SB_TQ, 2 * BLK), 0)
    jj = lax.broadcasted_iota(jnp.int32, (2 * BLK, 2 * BLK), 0) & (BLK - 1)
    ss = lax.broadcasted_iota(jnp.int32, (2 * BLK, 2 * BLK), 1)
    suffix_mat = jnp.where((jj > ss) | (ss >= BLK), 1.0, 0.0).astype(BF16)
    scale = HEAD_DIM ** -0.5

    def prep(j, _):
        rows = pl.ds(pl.multiple_of(j * BLK, BLK), BLK)
        for p in range(SB_PAIRS):
            kb = k_ref[0, rows, p * BLK:(p + 1) * BLK]
            vb = v_ref[0, rows, p * BLK:(p + 1) * BLK]
            kk_scr[p, j] = jnp.concatenate([jnp.where(lane < HEAD_DIM, kb, 0.0),
                                            jnp.where(lane >= HEAD_DIM, kb, 0.0)], axis=0).astype(BF16)
            vv_scr[p, j] = jnp.concatenate([jnp.where(lane < HEAD_DIM, vb, 0.0),
                                            jnp.where(lane >= HEAD_DIM, vb, 0.0)], axis=0).astype(BF16)
        return 0

    lax.fori_loop(0, nb, prep, 0)

    def key_blocks(qn, t, js, masked):
        first_row = [(len(js) - 1 - d) * BLK if masked else 0 for d in range(len(js))]
        chains = [(p, d) for p in range(SB_PAIRS) for d in range(len(js))]
        causal = [((js[d] * BLK + key_pos) < (t * SB_TQ + q_pos))[first_row[d]:] if masked else None
                  for d in range(len(js))]
        nz = {(p, d): _dot_nt(qn[p][first_row[d]:], kk_scr[p, js[d]]) for p, d in chains}
        log_beta, lhs = {}, {}
        for c in chains:
            log_keep = jnp.minimum(nz[c], 0.0) - jnp.log2(1.0 + jnp.exp2(-jnp.abs(nz[c])))
            log_beta[c] = log_keep - nz[c]
            if masked:
                log_keep = jnp.where(causal[c[1]], log_keep, 0.0)
            hi = log_keep.astype(BF16)
            lo = (log_keep - hi.astype(F32)).astype(BF16)
            lhs[c] = [jnp.concatenate([hi[:, h * BLK:(h + 1) * BLK], lo[:, h * BLK:(h + 1) * BLK]], axis=1)
                      for h in range(2)]
        cs = {c: [_dot(lhs[c][h], suffix_mat) for h in range(2)] for c in chains}
        w = {}
        for p in range(SB_PAIRS):
            carry = carry_scr[p]
            for d in range(len(js)):
                c, r0 = (p, d), first_row[d]
                between = jnp.concatenate([cs[c][0][:, :BLK], cs[c][1][:, :BLK]], axis=1)
                wc = jnp.exp2(log_beta[c] + between + carry[r0:])
                if masked:
                    wc = jnp.where(causal[d], wc, 0.0)
                w[c] = wc.astype(BF16)
                below = carry[r0:] + jnp.concatenate([cs[c][0][:, BLK:], cs[c][1][:, BLK:]], axis=1)
                carry = below if r0 == 0 else jnp.concatenate([carry[:r0], below], axis=0)
            carry_scr[p] = carry
        for p in range(SB_PAIRS):
            acc = acc_scr[p]
            for d, j in enumerate(js):
                r0 = first_row[d]
                below = acc[r0:] + _dot(w[(p, d)], vv_scr[p, j])
                acc = below if r0 == 0 else jnp.concatenate([acc[:r0], below], axis=0)
            acc_scr[p] = acc

    def q_tile(t, _):
        rows = pl.ds(pl.multiple_of(t * SB_TQ, SB_TQ), SB_TQ)
        qn = [(q_ref[0, rows, p * BLK:(p + 1) * BLK].astype(F32) * (-scale * LOG2E)).astype(BF16)
              for p in range(SB_PAIRS)]
        carry_scr[...] = jnp.zeros_like(carry_scr)
        acc_scr[...] = jnp.zeros_like(acc_scr)
        key_blocks(qn, t, [kpt * t + kpt - 1 - d for d in range(kpt)], True)

        def live(state):
            return (state[0] < kpt * t // SB_KPI) & (state[1] > SB_DEAD)

        def kv_step(state):
            i = state[0]
            key_blocks(qn, t, [kpt * t - 1 - i * SB_KPI - d for d in range(SB_KPI)], False)
            return i + 1, jnp.max(carry_scr[...])

        lax.while_loop(live, kv_step, (jnp.int32(0), jnp.float32(0.0)))
        for p in range(SB_PAIRS):
            o_ref[0, rows, p * BLK:(p + 1) * BLK] = acc_scr[p].astype(o_ref.dtype)
        return 0

    lax.fori_loop(0, nt, q_tile, 0)


def _sb(proj):
    b, s, _ = proj.shape
    assert (SB_TQ // BLK) % SB_KPI == 0
    width = SB_PAIRS * BLK
    q_col = 0
    k_col = SB_W // width
    v_col = 2 * SB_W // width
    return pl.pallas_call(
        _sb_kernel,
        grid=(b, SB_W // width),
        in_specs=[
            pl.BlockSpec((1, s, width), lambda i, p: (i, 0, q_col + p)),
            pl.BlockSpec((1, s, width), lambda i, p: (i, 0, k_col + p)),
            pl.BlockSpec((1, s, width), lambda i, p: (i, 0, v_col + p)),
        ],
        out_specs=pl.BlockSpec((1, s, width), lambda i, p: (i, 0, p)),
        out_shape=jax.ShapeDtypeStruct((b, s, SB_W), BF16),
        scratch_shapes=[pltpu.VMEM((SB_PAIRS, s // BLK, 2 * BLK, BLK), BF16),
                        pltpu.VMEM((SB_PAIRS, s // BLK, 2 * BLK, BLK), BF16),
                        pltpu.VMEM((SB_PAIRS, SB_TQ, 2 * BLK), F32), pltpu.VMEM((SB_PAIRS, SB_TQ, BLK), F32)],
        compiler_params=_params(("parallel", "parallel")),
        name="stick_breaking",
    )(proj, proj, proj)


def _hgrn_kernel(layer, lbl_ref, ng_ref, q_ref, f_ref, i_ref, gate_ref, o_ref, state_scr):
    nc = q_ref.shape[1] // HG_CHUNK
    logits = lbl_ref[...]
    e = jnp.exp(logits - jnp.max(logits, axis=0, keepdims=True))
    sm = e / jnp.sum(e, axis=0, keepdims=True)
    lb = jnp.sum(sm[1:layer + 1], axis=0, keepdims=True)
    ti = lax.broadcasted_iota(jnp.int32, (HG_CHUNK, HG_CHUNK), 0)
    si = lax.broadcasted_iota(jnp.int32, (HG_CHUNK, HG_CHUNK), 1)
    cumsum_mat = jnp.where(si <= ti, 1.0, 0.0).astype(BF16)
    row8 = lax.broadcasted_iota(jnp.int32, (SUBLANES, 1), 0)
    n_sub = HG_CHUNK // HG_SUB
    state_scr[...] = jnp.zeros_like(state_scr)

    def pair_rows(acc, qt, gt, ks, gs, vs, s):
        decay = jnp.exp2(gt - gs[s:s + 1, :])
        col = jnp.sum(qt * (ks[s:s + 1, :] * decay), axis=-1, keepdims=True)
        col = jnp.where(row8 >= s, col, 0.0)
        return acc + col * vs[s:s + 1, :]

    def chunk(c, _):
        rows = pl.ds(pl.multiple_of(c * HG_CHUNK, HG_CHUNK), HG_CHUNK)
        z = f_ref[0, rows, :]
        qx = q_ref[0, rows, :].astype(F32)
        v_all = i_ref[0, rows, :].astype(F32)
        gate = gate_ref[0, rows, :].astype(F32)
        sig = jax.nn.sigmoid(z)
        q_all = qx * jax.nn.sigmoid(qx)
        k_all = (1.0 - lb) * (1.0 - sig)
        log2_f = jnp.log2(lb + (1.0 - lb) * sig)
        g_all = _split_dot_left(cumsum_mat, log2_f)
        g_last_all = g_all[HG_CHUNK - 1:HG_CHUNK, :]
        qg_all = (q_all * jnp.exp2(g_all)).astype(BF16)
        kd_all = (k_all * jnp.exp2(g_last_all - g_all)).astype(BF16)
        gated = gate * jax.nn.sigmoid(gate)
        kt_all, qt_all = [], []
        for j in range(n_sub - 1):
            lo, hi = j * HG_SUB, (j + 1) * HG_SUB
            r = g_all[hi - 1:hi, :]
            kt_all.append((k_all[lo:hi] * jnp.exp2(r - g_all[lo:hi])).astype(BF16))
            qt_all.append((q_all[hi:] * jnp.exp2(g_all[hi:] - r)).astype(BF16))
        q_low, k_up, v_up = [], [], []
        for i in range(n_sub):
            lo, mid, hi = i * HG_SUB, i * HG_SUB + SUBLANES, (i + 1) * HG_SUB
            r = g_all[mid - 1:mid, :]
            q_low.append(q_all[mid:hi] * jnp.exp2(g_all[mid:hi] - r))
            k_up.append(k_all[lo:mid] * jnp.exp2(r - g_all[lo:mid]))
            v_up.append(v_all[lo:mid])
        q_low = jnp.concatenate(q_low, axis=0).astype(BF16)
        k_up = jnp.concatenate(k_up, axis=0).astype(BF16)
        v_up = jnp.concatenate(v_up, axis=0).astype(BF16)
        same_sub = (lax.broadcasted_iota(jnp.int32, (n_sub * SUBLANES, n_sub * SUBLANES), 0) // SUBLANES
                    == lax.broadcasted_iota(jnp.int32, (n_sub * SUBLANES, n_sub * SUBLANES), 1) // SUBLANES)
        heads = range(HG_HEADS)
        cols = [slice(h * HG_DK, (h + 1) * HG_DK) for h in heads]
        v_bf = v_all.astype(BF16)
        state = [state_scr[h] for h in heads]
        o_inter = [_dot_nt(qg_all[:, cols[h]], state[h].astype(BF16)) for h in heads]
        sc = [[_dot_nt(qt_all[j][:, cols[h]], kt_all[j][:, cols[h]]).astype(BF16)
               for j in range(n_sub - 1)] for h in heads]
        contrib = [[_dot(sc[h][j], v_bf[j * HG_SUB:(j + 1) * HG_SUB, cols[h]])
                    for j in range(n_sub - 1)] for h in heads]
        new_state = [state[h] * jnp.exp2(g_last_all[:, cols[h]]) + _dot_tn(v_bf[:, cols[h]], kd_all[:, cols[h]])
                     for h in heads]
        sc_low = [jnp.where(same_sub, _dot_nt(q_low[:, cols[h]], k_up[:, cols[h]]), 0.0).astype(BF16) for h in heads]
        contrib_low = [_dot(sc_low[h], v_up[:, cols[h]]) for h in heads]
        tiles = [[None] * (HG_CHUNK // SUBLANES) for _ in heads]
        for h in heads:
            for i in range(n_sub):
                part = jnp.zeros((HG_SUB, HG_DK), F32)
                for j in range(i):
                    part = part + contrib[h][j][(i - j - 1) * HG_SUB:(i - j) * HG_SUB]
                tiles[h][2 * i] = part[:SUBLANES]
                tiles[h][2 * i + 1] = part[SUBLANES:] + contrib_low[h][i * SUBLANES:(i + 1) * SUBLANES]
        for i in range(2 * n_sub):
            a = slice(i * SUBLANES, (i + 1) * SUBLANES)
            for s in range(SUBLANES):
                for h in heads:
                    g, q, k, v = g_all[:, cols[h]], q_all[:, cols[h]], k_all[:, cols[h]], v_all[:, cols[h]]
                    tiles[h][i] = pair_rows(tiles[h][i], q[a], g[a], k[a], g[a], v[a], s)
        for h in heads:
            o = o_inter[h] + jnp.concatenate(tiles[h], axis=0)
            state_scr[h] = new_state[h]
            o_ref[0, rows, cols[h]] = (_rms(o, ng_ref[...]) * gated[:, cols[h]]).astype(o_ref.dtype)
        return 0

    lax.fori_loop(0, nc, chunk, 0, unroll=HG_UNROLL)


def _hgrn(proj, f_logit, lb_logits, norm_g, layer):
    b, s, _ = proj.shape
    depth = lb_logits.shape[0]
    assert HG_SUB == 2 * SUBLANES
    blk = lambda off: pl.BlockSpec((1, s, HG_WIDTH), lambda i: (i, 0, off))
    return pl.pallas_call(
        functools.partial(_hgrn_kernel, layer),
        grid=(b,),
        in_specs=[
            pl.BlockSpec((depth, HG_WIDTH), lambda i: (0, 0)),
            pl.BlockSpec((1, HG_DK), lambda i: (0, 0)),
            blk(0), blk(0), blk(2), blk(3),
        ],
        out_specs=pl.BlockSpec((1, s, HG_WIDTH), lambda i: (i, 0, 0)),
        out_shape=jax.ShapeDtypeStruct((b, s, HG_WIDTH), BF16),
        scratch_shapes=[pltpu.VMEM((HG_HEADS, HG_DK, HG_DK), F32)],
        compiler_params=_params(("parallel",)),
        name="hgrn2",
    )(lb_logits, norm_g.reshape(1, HG_DK), proj, f_logit, proj, proj)


def _attn_columns(w_in):
    split = SWA_Q + 2 * SWA_KV
    return jnp.concatenate([w_in[:, split:], w_in[:, :split]], axis=1)


def kernel(x, norm_g, ffn_w_in, ffn_w_out, attn_w_in, attn_sinks, attn_w_out, rec_w_in,
           hgrn_lb_logits, hgrn_norm_g, conv_w, rec_w_out, final_g):
    b, s, d = x.shape
    depth = norm_g.shape[0]
    m = b * s
    h = x.reshape(m, d)
    for layer in range(depth):
        h = _ffn(h, norm_g[layer, 0], ffn_w_in[layer, 0].astype(BF16), ffn_w_out[layer, 0].astype(BF16),
                 final_g, final_norm=False)
        if layer % 2 == 0:
            e = layer // 2
            proj = _proj_in(h, norm_g[layer, 1], _attn_columns(attn_w_in[e]).astype(BF16)).reshape(b, s, ATTN_IN)
            o_a = _swa(proj, attn_sinks[e])
            o_b = _sb(proj)
            mix = (o_a.reshape(m, SWA_Q), o_b.reshape(m, SB_W), attn_w_out[e].astype(BF16))
        else:
            o = layer // 2
            proj, f_logit, o_d = _proj_rec(h, norm_g[layer, 1], rec_w_in[o].astype(BF16), conv_w[o], s)
            o_c = _hgrn(proj.reshape(b, s, 4 * HG_WIDTH), f_logit.reshape(b, s, HG_WIDTH), hgrn_lb_logits,
                        hgrn_norm_g[o], layer)
            mix = (o_c.reshape(m, HG_WIDTH), o_d, rec_w_out[o].astype(BF16))
        h = _ffn(h, norm_g[layer, 2], ffn_w_in[layer, 1].astype(BF16), ffn_w_out[layer, 1].astype(BF16),
                 final_g, final_norm=(layer == depth - 1), mix=mix)
    return h.reshape(b, s, d)
```

```python
import functools
import math

import jax
import jax.numpy as jnp
from jax import lax
from jax.experimental import pallas as pl
from jax.experimental.pallas import tpu as pltpu

F32 = jnp.float32
BF16 = jnp.bfloat16

D_MODEL = 1024
HEAD_DIM = 64
SWA_HEADS = 8
SWA_KV_HEADS = 2
SWA_GROUP = SWA_HEADS // SWA_KV_HEADS
WINDOW = 128
BLK = 128
SUBLANES = 8
SB_HEADS = 8
HG_WIDTH = 512
HG_DK = 128
HG_HEADS = HG_WIDTH // HG_DK
HG_CHUNK = 64
HG_SUB = 16
HG_UNROLL = 4
SWA_UNROLL = 4
CONV_WIDTH = 512
CONV_K = 3
D_FF = 2816
EPS = 1e-6

SWA_Q = SWA_HEADS * HEAD_DIM
SWA_KV = SWA_KV_HEADS * HEAD_DIM
SB_W = SB_HEADS * HEAD_DIM
ATTN_IN = SWA_Q + 2 * SWA_KV + 3 * SB_W
REC_IN = 4 * HG_WIDTH + 3 * CONV_WIDTH

VMEM_LIMIT_BYTES = 56 * 1024 * 1024


def _params(semantics):
    return pltpu.CompilerParams(dimension_semantics=semantics,
                                vmem_limit_bytes=VMEM_LIMIT_BYTES)


def _rms(x, g):
    return x * lax.rsqrt(jnp.mean(x * x, axis=-1, keepdims=True) + EPS) * g


def _dot(a, b):
    return jnp.dot(a, b, preferred_element_type=F32)


def _dot_nt(a, b):
    return lax.dot_general(a, b, (((1,), (1,)), ((), ())), preferred_element_type=F32)


def _dot_tn(a, b):
    return lax.dot_general(a, b, (((0,), (0,)), ((), ())), preferred_element_type=F32)


def _split_dot(x, m01):
    hi = x.astype(BF16)
    lo = (x - hi.astype(F32)).astype(BF16)
    return _dot(hi, m01) + _dot(lo, m01)


def _split_dot_left(m01, x):
    hi = x.astype(BF16)
    lo = (x - hi.astype(F32)).astype(BF16)
    return _dot(m01, hi) + _dot(m01, lo)


def _ffn_kernel(*refs, final_norm, with_mix):
    if with_mix:
        x_ref, a_ref, b_ref, wm_ref, g_ref, wi_ref, wo_ref, fg_ref, o_ref = refs
    else:
        x_ref, g_ref, wi_ref, wo_ref, fg_ref, o_ref = refs
    d_ff = wo_ref.shape[0]
    x = x_ref[...]
    if with_mix:
        ka = a_ref.shape[1]
        x = x + _dot(a_ref[...], wm_ref[:ka, :]) + _dot(b_ref[...], wm_ref[ka:, :])
    n = _rms(x, g_ref[...]).astype(BF16)
    gate = _dot(n, wi_ref[:, :d_ff])
    up = _dot(n, wi_ref[:, d_ff:])
    act = (gate * jax.nn.sigmoid(gate) * up).astype(BF16)
    h = x + 0.5 * _dot(act, wo_ref[...])
    if final_norm:
        h = _rms(h, fg_ref[...])
    o_ref[...] = h


FFN_TM = 512


def _ffn(h, g, w_in_all, w_out_all, layer, which, final_g, *, final_norm, mix=None):
    m, d = h.shape
    d_ff = w_out_all.shape[2]
    tm = min(FFN_TM, m)
    rows = lambda width: pl.BlockSpec((tm, width), lambda i: (i, 0))
    resident = lambda shape: pl.BlockSpec(shape, lambda i: (0, 0), pipeline_mode=pl.Buffered(1))
    stacked = lambda shape: pl.BlockSpec((None, None) + shape, lambda i: (layer, which, 0, 0),
                                         pipeline_mode=pl.Buffered(1))
    operands, in_specs = [h], [rows(d)]
    if mix is not None:
        a, b, w_mix = mix
        operands += [a, b, w_mix]
        in_specs += [rows(a.shape[1]), rows(b.shape[1]), resident(w_mix.shape)]
    operands += [g.reshape(1, d), w_in_all, w_out_all, final_g.reshape(1, d)]
    in_specs += [resident((1, d)), stacked((d, 2 * d_ff)), stacked((d_ff, d)), resident((1, d))]
    return pl.pallas_call(
        functools.partial(_ffn_kernel, final_norm=final_norm, with_mix=mix is not None),
        grid=(m // tm,),
        in_specs=in_specs,
        out_specs=rows(d),
        out_shape=jax.ShapeDtypeStruct((m, d), F32),
        compiler_params=_params(("parallel",)),
        name="ffn",
    )(*operands)


def _proj_in_kernel(x_ref, g_ref, w_ref, o_ref):
    o_ref[...] = _dot(_rms(x_ref[...], g_ref[...]).astype(BF16), w_ref[...]).astype(o_ref.dtype)


PROJ_TM = 512


def _proj_in(h, g, w):
    m, d = h.shape
    n_out = w.shape[1]
    tm = min(PROJ_TM, m)
    resident = lambda shape: pl.BlockSpec(shape, lambda i: (0, 0), pipeline_mode=pl.Buffered(1))
    return pl.pallas_call(
        _proj_in_kernel,
        grid=(m // tm,),
        in_specs=[pl.BlockSpec((tm, d), lambda i: (i, 0)), resident((1, d)), resident((d, n_out))],
        out_specs=pl.BlockSpec((tm, n_out), lambda i: (i, 0)),
        out_shape=jax.ShapeDtypeStruct((m, n_out), BF16),
        compiler_params=_params(("parallel",)),
        name="proj_in",
    )(h, g.reshape(1, d), w)


def _proj_rec_kernel(x_ref, g_ref, w_ref, cw_ref, o_ref, f_ref, od_ref, tail_scr, *, seq_tiles):
    y = _dot(_rms(x_ref[...], g_ref[...]).astype(BF16), w_ref[...])
    hg = 4 * HG_WIDTH
    o_ref[...] = y[:, :hg].astype(o_ref.dtype)
    f_ref[...] = y[:, HG_WIDTH:2 * HG_WIDTH]
    gate_b = y[:, hg:hg + CONV_WIDTH]
    x = y[:, hg + CONV_WIDTH:hg + 2 * CONV_WIDTH] * y[:, hg + 2 * CONV_WIDTH:]
    first = pl.program_id(0) % seq_tiles == 0
    tail = jnp.where(first, 0.0, tail_scr[...])
    row = lax.broadcasted_iota(jnp.int32, x.shape, 0)
    t1 = tail[SUBLANES - 1:SUBLANES, :]
    t2 = tail[SUBLANES - 2:SUBLANES - 1, :]
    x1 = jnp.where(row == 0, t1, pltpu.roll(x, 1, 0))
    x2 = jnp.where(row == 0, t2, jnp.where(row == 1, t1, pltpu.roll(x, 2, 0)))
    conv = cw_ref[0:1, :] * x2 + cw_ref[1:2, :] * x1 + cw_ref[2:3, :] * x
    od_ref[...] = (gate_b * conv).astype(od_ref.dtype)
    tail_scr[...] = x[x.shape[0] - SUBLANES:, :]


def _proj_rec(h, g, w, conv_w, seq_len):
    m, d = h.shape
    tm = min(PROJ_TM, seq_len)
    assert CONV_K - 1 <= SUBLANES and seq_len % tm == 0
    resident = lambda shape: pl.BlockSpec(shape, lambda i: (0, 0), pipeline_mode=pl.Buffered(1))
    rows = lambda width: pl.BlockSpec((tm, width), lambda i: (i, 0))
    return pl.pallas_call(
        functools.partial(_proj_rec_kernel, seq_tiles=seq_len // tm),
        grid=(m // tm,),
        in_specs=[rows(d), resident((1, d)), resident((d, REC_IN)), resident((CONV_K, CONV_WIDTH))],
        out_specs=[rows(4 * HG_WIDTH), rows(HG_WIDTH), rows(CONV_WIDTH)],
        out_shape=[jax.ShapeDtypeStruct((m, 4 * HG_WIDTH), BF16), jax.ShapeDtypeStruct((m, HG_WIDTH), F32),
                   jax.ShapeDtypeStruct((m, CONV_WIDTH), BF16)],
        scratch_shapes=[pltpu.VMEM((SUBLANES, CONV_WIDTH), F32)],
        compiler_params=_params(("arbitrary",)),
        name="proj_rec",
    )(h, g.reshape(1, d), w, conv_w)


def _swa_kernel(sink_ref, q_ref, k_ref, v_ref, o_ref, k_swapped, v_swapped):
    nb = q_ref.shape[1] // BLK
    half = HEAD_DIM
    k_swapped[...] = pltpu.roll(k_ref[0].astype(F32), half, 1).astype(BF16)
    v_swapped[...] = pltpu.roll(v_ref[0].astype(F32), half, 1).astype(BF16)
    qi = lax.broadcasted_iota(jnp.int32, (BLK, 2 * BLK), 0)
    kj = lax.broadcasted_iota(jnp.int32, (BLK, 2 * BLK), 1)
    dist = qi + BLK - kj
    in_window = (dist >= 0) & (dist < WINDOW)
    dist_f = dist.astype(F32)
    lane = lax.broadcasted_iota(jnp.int32, (BLK, BLK), 1)
    scale = HEAD_DIM ** -0.5

    def q_block(n, _):
        cur = pl.ds(pl.multiple_of(n * BLK, BLK), BLK)
        prev = pl.ds(pl.multiple_of(jnp.maximum(n - 1, 0) * BLK, BLK), BLK)
        valid = in_window & (n * BLK - BLK + kj >= 0)
        penalty = jnp.where(valid, dist_f, jnp.inf)
        band = lambda ref: jnp.concatenate([ref[prev, :], ref[cur, :]], axis=0)
        k_by_half = (band(k_ref.at[0]), band(k_swapped))
        v_by_half = (band(v_ref.at[0]), band(v_swapped))
        heads = range(SWA_HEADS)
        kv_half = [(h // SWA_GROUP) ^ (h % 2) for h in heads]
        scores = []
        for h in heads:
            q_col = q_ref[0, cur, (h // 2) * BLK:(h // 2 + 1) * BLK].astype(F32) * scale
            in_half = (lane >= (h % 2) * half) & (lane < (h % 2 + 1) * half)
            qm = jnp.where(in_half, q_col, 0.0).astype(BF16)
            slope = 2.0 ** (-8.0 * (h + 1) / SWA_HEADS)
            scores.append(_dot_nt(qm, k_by_half[kv_half[h]]) - slope * penalty)
        row_max = [jnp.maximum(jnp.max(scores[h], axis=-1, keepdims=True), sink_ref[h]) for h in heads]
        probs = [jnp.exp(scores[h] - row_max[h]) for h in heads]
        num = [_dot(probs[h].astype(BF16), v_by_half[kv_half[h]]) for h in heads]
        den = [jnp.sum(probs[h], axis=-1, keepdims=True) + jnp.exp(sink_ref[h] - row_max[h]) for h in heads]
        out = [num[h] / den[h] for h in heads]
        for col in range(SWA_Q // BLK):
            o_ref[0, cur, col * BLK:(col + 1) * BLK] = jnp.where(
                lane < half, out[2 * col], out[2 * col + 1]).astype(o_ref.dtype)
        return 0

    lax.fori_loop(0, nb, q_block, 0, unroll=SWA_UNROLL)


def _swa(proj, sinks):
    b, s, _ = proj.shape
    assert SWA_KV == BLK and SWA_KV_HEADS == 2
    q_col = 3 * SB_W // SWA_Q
    k_col = (3 * SB_W + SWA_Q) // SWA_KV
    v_col = k_col + 1
    return pl.pallas_call(
        _swa_kernel,
        grid=(b,),
        in_specs=[
            pl.BlockSpec(memory_space=pltpu.SMEM),
            pl.BlockSpec((1, s, SWA_Q), lambda i: (i, 0, q_col)),
            pl.BlockSpec((1, s, SWA_KV), lambda i: (i, 0, k_col)),
            pl.BlockSpec((1, s, SWA_KV), lambda i: (i, 0, v_col)),
        ],
        out_specs=pl.BlockSpec((1, s, SWA_Q), lambda i: (i, 0, 0)),
        out_shape=jax.ShapeDtypeStruct((b, s, SWA_Q), BF16),
        scratch_shapes=[pltpu.VMEM((s, SWA_KV), BF16), pltpu.VMEM((s, SWA_KV), BF16)],
        compiler_params=_params(("parallel",)),
        name="swa",
    )(sinks, proj, proj, proj)


SB_TQ = 256
SB_KPI = 2
LOG2E = math.log2(math.e)
SB_DEAD = -256.0


SB_PAIRS = 4


def _sb_kernel(q_ref, k_ref, v_ref, o_ref, kk_scr, vv_scr, carry_scr, acc_scr):
    s = q_ref.shape[1]
    nt = s // SB_TQ
    nb = s // BLK
    kpt = SB_TQ // BLK
    lane = lax.broadcasted_iota(jnp.int32, (BLK, BLK), 1)
    key_pos = lax.broadcasted_iota(jnp.int32, (SB_TQ, 2 * BLK), 1) & (BLK - 1)
    q_pos = lax.broadcasted_iota(jnp.int32, (SB_TQ, 2 * BLK), 0)
    jj = lax.broadcasted_iota(jnp.int32, (2 * BLK, 2 * BLK), 0) & (BLK - 1)
    ss = lax.broadcasted_iota(jnp.int32, (2 * BLK, 2 * BLK), 1)
    suffix_mat = jnp.where((jj > ss) | (ss >= BLK), 1.0, 0.0).astype(BF16)
    scale = HEAD_DIM ** -0.5

    def prep(j, _):
        rows = pl.ds(pl.multiple_of(j * BLK, BLK), BLK)
        for p in range(SB_PAIRS):
            kb = k_ref[0, rows, p * BLK:(p + 1) * BLK]
            vb = v_ref[0, rows, p * BLK:(p + 1) * BLK]
            kk_scr[p, j] = jnp.concatenate([jnp.where(lane < HEAD_DIM, kb, 0.0),
                                            jnp.where(lane >= HEAD_DIM, kb, 0.0)], axis=0).astype(BF16)
            vv_scr[p, j] = jnp.concatenate([jnp.where(lane < HEAD_DIM, vb, 0.0),
                                            jnp.where(lane >= HEAD_DIM, vb, 0.0)], axis=0).astype(BF16)
        return 0

    lax.fori_loop(0, nb, prep, 0)

    def key_blocks(qn, t, js, masked):
        first_row = [(len(js) - 1 - d) * BLK if masked else 0 for d in range(len(js))]
        chains = [(p, d) for p in range(SB_PAIRS) for d in range(len(js))]
        causal = [((js[d] * BLK + key_pos) < (t * SB_TQ + q_pos))[first_row[d]:] if masked else None
                  for d in range(len(js))]
        nz = {(p, d): _dot_nt(qn[p][first_row[d]:], kk_scr[p, js[d]]) for p, d in chains}
        log_beta, lhs = {}, {}
        for c in chains:
            log_keep = jnp.minimum(nz[c], 0.0) - jnp.log2(1.0 + jnp.exp2(-jnp.abs(nz[c])))
            log_beta[c] = log_keep - nz[c]
            if masked:
                log_keep = jnp.where(causal[c[1]], log_keep, 0.0)
            hi = log_keep.astype(BF16)
            lo = (log_keep - hi.astype(F32)).astype(BF16)
            lhs[c] = [jnp.concatenate([hi[:, h * BLK:(h + 1) * BLK], lo[:, h * BLK:(h + 1) * BLK]], axis=1)
                      for h in range(2)]
        cs = {c: [_dot(lhs[c][h], suffix_mat) for h in range(2)] for c in chains}
        w = {}
        for p in range(SB_PAIRS):
            carry = carry_scr[p]
            for d in range(len(js)):
                c, r0 = (p, d), first_row[d]
                between = jnp.concatenate([cs[c][0][:, :BLK], cs[c][1][:, :BLK]], axis=1)
                wc = jnp.exp2(log_beta[c] + between + carry[r0:])
                if masked:
                    wc = jnp.where(causal[d], wc, 0.0)
                w[c] = wc.astype(BF16)
                below = carry[r0:] + jnp.concatenate([cs[c][0][:, BLK:], cs[c][1][:, BLK:]], axis=1)
                carry = below if r0 == 0 else jnp.concatenate([carry[:r0], below], axis=0)
            carry_scr[p] = carry
        for p in range(SB_PAIRS):
            acc = acc_scr[p]
            for d, j in enumerate(js):
                r0 = first_row[d]
                below = acc[r0:] + _dot(w[(p, d)], vv_scr[p, j])
                acc = below if r0 == 0 else jnp.concatenate([acc[:r0], below], axis=0)
            acc_scr[p] = acc

    def q_tile(t, _):
        rows = pl.ds(pl.multiple_of(t * SB_TQ, SB_TQ), SB_TQ)
        qn = [(q_ref[0, rows, p * BLK:(p + 1) * BLK].astype(F32) * (-scale * LOG2E)).astype(BF16)
              for p in range(SB_PAIRS)]
        carry_scr[...] = jnp.zeros_like(carry_scr)
        acc_scr[...] = jnp.zeros_like(acc_scr)
        key_blocks(qn, t, [kpt * t + kpt - 1 - d for d in range(kpt)], True)

        def live(state):
            return (state[0] < kpt * t // SB_KPI) & (state[1] > SB_DEAD)

        def kv_step(state):
            i = state[0]
            key_blocks(qn, t, [kpt * t - 1 - i * SB_KPI - d for d in range(SB_KPI)], False)
            return i + 1, jnp.max(carry_scr[...])

        lax.while_loop(live, kv_step, (jnp.int32(0), jnp.float32(0.0)))
        for p in range(SB_PAIRS):
            o_ref[0, rows, p * BLK:(p + 1) * BLK] = acc_scr[p].astype(o_ref.dtype)
        return 0

    lax.fori_loop(0, nt, q_tile, 0)


def _sb(proj):
    b, s, _ = proj.shape
    assert (SB_TQ // BLK) % SB_KPI == 0
    width = SB_PAIRS * BLK
    q_col = 0
    k_col = SB_W // width
    v_col = 2 * SB_W // width
    return pl.pallas_call(
        _sb_kernel,
        grid=(b, SB_W // width),
        in_specs=[
            pl.BlockSpec((1, s, width), lambda i, p: (i, 0, q_col + p)),
            pl.BlockSpec((1, s, width), lambda i, p: (i, 0, k_col + p)),
            pl.BlockSpec((1, s, width), lambda i, p: (i, 0, v_col + p)),
        ],
        out_specs=pl.BlockSpec((1, s, width), lambda i, p: (i, 0, p)),
        out_shape=jax.ShapeDtypeStruct((b, s, SB_W), BF16),
        scratch_shapes=[pltpu.VMEM((SB_PAIRS, s // BLK, 2 * BLK, BLK), BF16),
                        pltpu.VMEM((SB_PAIRS, s // BLK, 2 * BLK, BLK), BF16),
                        pltpu.VMEM((SB_PAIRS, SB_TQ, 2 * BLK), F32), pltpu.VMEM((SB_PAIRS, SB_TQ, BLK), F32)],
        compiler_params=_params(("parallel", "parallel")),
        name="stick_breaking",
    )(proj, proj, proj)


def _hgrn_kernel(layer, lbl_ref, ng_ref, q_ref, f_ref, i_ref, gate_ref, o_ref, state_scr):
    nc = q_ref.shape[1] // HG_CHUNK
    logits = lbl_ref[...]
    e = jnp.exp(logits - jnp.max(logits, axis=0, keepdims=True))
    sm = e / jnp.sum(e, axis=0, keepdims=True)
    lb = jnp.sum(sm[1:layer + 1], axis=0, keepdims=True)
    ti = lax.broadcasted_iota(jnp.int32, (HG_CHUNK, HG_CHUNK), 0)
    si = lax.broadcasted_iota(jnp.int32, (HG_CHUNK, HG_CHUNK), 1)
    cumsum_mat = jnp.where(si <= ti, 1.0, 0.0).astype(BF16)
    row8 = lax.broadcasted_iota(jnp.int32, (SUBLANES, 1), 0)
    n_sub = HG_CHUNK // HG_SUB
    state_scr[...] = jnp.zeros_like(state_scr)

    def pair_rows(acc, qt, gt, ks, gs, vs, s):
        decay = jnp.exp2(gt - gs[s:s + 1, :])
        col = jnp.sum(qt * (ks[s:s + 1, :] * decay), axis=-1, keepdims=True)
        col = jnp.where(row8 >= s, col, 0.0)
        return acc + col * vs[s:s + 1, :]

    def chunk(c, _):
        rows = pl.ds(pl.multiple_of(c * HG_CHUNK, HG_CHUNK), HG_CHUNK)
        z = f_ref[0, rows, :]
        qx = q_ref[0, rows, :].astype(F32)
        v_all = i_ref[0, rows, :].astype(F32)
        gate = gate_ref[0, rows, :].astype(F32)
        sig = jax.nn.sigmoid(z)
        q_all = qx * jax.nn.sigmoid(qx)
        k_all = (1.0 - lb) * (1.0 - sig)
        log2_f = jnp.log2(lb + (1.0 - lb) * sig)
        g_all = _split_dot_left(cumsum_mat, log2_f)
        g_last_all = g_all[HG_CHUNK - 1:HG_CHUNK, :]
        qg_all = (q_all * jnp.exp2(g_all)).astype(BF16)
        kd_all = (k_all * jnp.exp2(g_last_all - g_all)).astype(BF16)
        gated = gate * jax.nn.sigmoid(gate)
        kt_all, qt_all = [], []
        for j in range(n_sub - 1):
            lo, hi = j * HG_SUB, (j + 1) * HG_SUB
            r = g_all[hi - 1:hi, :]
            kt_all.append((k_all[lo:hi] * jnp.exp2(r - g_all[lo:hi])).astype(BF16))
            qt_all.append((q_all[hi:] * jnp.exp2(g_all[hi:] - r)).astype(BF16))
        q_low, k_up, v_up = [], [], []
        for i in range(n_sub):
            lo, mid, hi = i * HG_SUB, i * HG_SUB + SUBLANES, (i + 1) * HG_SUB
            r = g_all[mid - 1:mid, :]
            q_low.append(q_all[mid:hi] * jnp.exp2(g_all[mid:hi] - r))
            k_up.append(k_all[lo:mid] * jnp.exp2(r - g_all[lo:mid]))
            v_up.append(v_all[lo:mid])
        q_low = jnp.concatenate(q_low, axis=0).astype(BF16)
        k_up = jnp.concatenate(k_up, axis=0).astype(BF16)
        v_up = jnp.concatenate(v_up, axis=0).astype(BF16)
        same_sub = (lax.broadcasted_iota(jnp.int32, (n_sub * SUBLANES, n_sub * SUBLANES), 0) // SUBLANES
                    == lax.broadcasted_iota(jnp.int32, (n_sub * SUBLANES, n_sub * SUBLANES), 1) // SUBLANES)
        heads = range(HG_HEADS)
        cols = [slice(h * HG_DK, (h + 1) * HG_DK) for h in heads]
        v_bf = v_all.astype(BF16)
        state = [state_scr[h] for h in heads]
        o_inter = [_dot_nt(qg_all[:, cols[h]], state[h].astype(BF16)) for h in heads]
        sc = [[_dot_nt(qt_all[j][:, cols[h]], kt_all[j][:, cols[h]]).astype(BF16)
               for j in range(n_sub - 1)] for h in heads]
        contrib = [[_dot(sc[h][j], v_bf[j * HG_SUB:(j + 1) * HG_SUB, cols[h]])
                    for j in range(n_sub - 1)] for h in heads]
        new_state = [state[h] * jnp.exp2(g_last_all[:, cols[h]]) + _dot_tn(v_bf[:, cols[h]], kd_all[:, cols[h]])
                     for h in heads]
        sc_low = [jnp.where(same_sub, _dot_nt(q_low[:, cols[h]], k_up[:, cols[h]]), 0.0).astype(BF16) for h in heads]
        contrib_low = [_dot(sc_low[h], v_up[:, cols[h]]) for h in heads]
        tiles = [[None] * (HG_CHUNK // SUBLANES) for _ in heads]
        for h in heads:
            for i in range(n_sub):
                part = jnp.zeros((HG_SUB, HG_DK), F32)
                for j in range(i):
                    part = part + contrib[h][j][(i - j - 1) * HG_SUB:(i - j) * HG_SUB]
                tiles[h][2 * i] = part[:SUBLANES]
                tiles[h][2 * i + 1] = part[SUBLANES:] + contrib_low[h][i * SUBLANES:(i + 1) * SUBLANES]
        for i in range(2 * n_sub):
            a = slice(i * SUBLANES, (i + 1) * SUBLANES)
            for s in range(SUBLANES):
                for h in heads:
                    g, q, k, v = g_all[:, cols[h]], q_all[:, cols[h]], k_all[:, cols[h]], v_all[:, cols[h]]
                    tiles[h][i] = pair_rows(tiles[h][i], q[a], g[a], k[a], g[a], v[a], s)
        for h in heads:
            o = o_inter[h] + jnp.concatenate(tiles[h], axis=0)
            state_scr[h] = new_state[h]
            o_ref[0, rows, cols[h]] = (_rms(o, ng_ref[...]) * gated[:, cols[h]]).astype(o_ref.dtype)
        return 0

    lax.fori_loop(0, nc, chunk, 0, unroll=HG_UNROLL)


def _hgrn(proj, f_logit, lb_logits, norm_g, layer):
    b, s, _ = proj.shape
    depth = lb_logits.shape[0]
    assert HG_SUB == 2 * SUBLANES
    blk = lambda off: pl.BlockSpec((1, s, HG_WIDTH), lambda i: (i, 0, off))
    return pl.pallas_call(
        functools.partial(_hgrn_kernel, layer),
        grid=(b,),
        in_specs=[
            pl.BlockSpec((depth, HG_WIDTH), lambda i: (0, 0)),
            pl.BlockSpec((1, HG_DK), lambda i: (0, 0)),
            blk(0), blk(0), blk(2), blk(3),
        ],
        out_specs=pl.BlockSpec((1, s, HG_WIDTH), lambda i: (i, 0, 0)),
        out_shape=jax.ShapeDtypeStruct((b, s, HG_WIDTH), BF16),
        scratch_shapes=[pltpu.VMEM((HG_HEADS, HG_DK, HG_DK), F32)],
        compiler_params=_params(("parallel",)),
        name="hgrn2",
    )(lb_logits, norm_g.reshape(1, HG_DK), proj, f_logit, proj, proj)


def _attn_columns(w_in):
    split = SWA_Q + 2 * SWA_KV
    return jnp.concatenate([w_in[:, split:], w_in[:, :split]], axis=1)


def kernel(x, norm_g, ffn_w_in, ffn_w_out, attn_w_in, attn_sinks, attn_w_out, rec_w_in,
           hgrn_lb_logits, hgrn_norm_g, conv_w, rec_w_out, final_g):
    b, s, d = x.shape
    depth = norm_g.shape[0]
    m = b * s
    h = x.reshape(m, d)
    w_in_bf, w_out_bf = ffn_w_in.astype(BF16), ffn_w_out.astype(BF16)
    for layer in range(depth):
        h = _ffn(h, norm_g[layer, 0], w_in_bf, w_out_bf, layer, 0, final_g, final_norm=False)
        if layer % 2 == 0:
            e = layer // 2
            proj = _proj_in(h, norm_g[layer, 1], _attn_columns(attn_w_in[e]).astype(BF16)).reshape(b, s, ATTN_IN)
            o_a = _swa(proj, attn_sinks[e])
            o_b = _sb(proj)
            mix = (o_a.reshape(m, SWA_Q), o_b.reshape(m, SB_W), attn_w_out[e].astype(BF16))
        else:
            o = layer // 2
            proj, f_logit, o_d = _proj_rec(h, norm_g[layer, 1], rec_w_in[o].astype(BF16), conv_w[o], s)
            o_c = _hgrn(proj.reshape(b, s, 4 * HG_WIDTH), f_logit.reshape(b, s, HG_WIDTH), hgrn_lb_logits,
                        hgrn_norm_g[o], layer)
            mix = (o_c.reshape(m, HG_WIDTH), o_d, rec_w_out[o].astype(BF16))
        h = _ffn(h, norm_g[layer, 2], w_in_bf, w_out_bf, layer, 1, final_g,
                 final_norm=(layer == depth - 1), mix=mix)
    return h.reshape(b, s, d)
```

```python
import functools
import math

import jax
import jax.numpy as jnp
from jax import lax
from jax.experimental import pallas as pl
from jax.experimental.pallas import tpu as pltpu

F32 = jnp.float32
BF16 = jnp.bfloat16

D_MODEL = 1024
HEAD_DIM = 64
SWA_HEADS = 8
SWA_KV_HEADS = 2
SWA_GROUP = SWA_HEADS // SWA_KV_HEADS
WINDOW = 128
BLK = 128
SUBLANES = 8
SB_HEADS = 8
HG_WIDTH = 512
HG_DK = 128
HG_HEADS = HG_WIDTH // HG_DK
HG_CHUNK = 64
HG_SUB = 16
HG_UNROLL = 4
HG_UNROLL_SINGLE = 8
HG_SAFE_DECAY = 200.0
SWA_UNROLL = 4
CONV_WIDTH = 512
CONV_K = 3
D_FF = 2816
EPS = 1e-6

SWA_Q = SWA_HEADS * HEAD_DIM
SWA_KV = SWA_KV_HEADS * HEAD_DIM
SB_W = SB_HEADS * HEAD_DIM
ATTN_IN = SWA_Q + 2 * SWA_KV + 3 * SB_W
REC_IN = 4 * HG_WIDTH + 3 * CONV_WIDTH

VMEM_LIMIT_BYTES = 56 * 1024 * 1024


def _params(semantics):
    return pltpu.CompilerParams(dimension_semantics=semantics,
                                vmem_limit_bytes=VMEM_LIMIT_BYTES)


def _rms(x, g):
    return x * lax.rsqrt(jnp.mean(x * x, axis=-1, keepdims=True) + EPS) * g


def _sigmoid(x):
    return 0.5 * (1.0 + jnp.tanh(0.5 * x))


def _dot(a, b):
    return jnp.dot(a, b, preferred_element_type=F32)


def _dot_nt(a, b):
    return lax.dot_general(a, b, (((1,), (1,)), ((), ())), preferred_element_type=F32)


def _dot_tn(a, b):
    return lax.dot_general(a, b, (((0,), (0,)), ((), ())), preferred_element_type=F32)


def _split_dot(x, m01):
    hi = x.astype(BF16)
    lo = (x - hi.astype(F32)).astype(BF16)
    return _dot(hi, m01) + _dot(lo, m01)


def _split_dot_left(m01, x):
    hi = x.astype(BF16)
    lo = (x - hi.astype(F32)).astype(BF16)
    return _dot(m01, hi) + _dot(m01, lo)


def _ffn_kernel(*refs, final_norm, with_mix):
    if with_mix:
        x_ref, a_ref, b_ref, wm_ref, g_ref, wi_ref, wo_ref, fg_ref, o_ref = refs
    else:
        x_ref, g_ref, wi_ref, wo_ref, fg_ref, o_ref = refs
    d_ff = wo_ref.shape[0]
    x = x_ref[...]
    if with_mix:
        ka = a_ref.shape[1]
        x = x + _dot(a_ref[...], wm_ref[:ka, :]) + _dot(b_ref[...], wm_ref[ka:, :])
    n = _rms(x, g_ref[...]).astype(BF16)
    gate = _dot(n, wi_ref[:, :d_ff])
    up = _dot(n, wi_ref[:, d_ff:])
    act = (gate * jax.nn.sigmoid(gate) * up).astype(BF16)
    h = x + 0.5 * _dot(act, wo_ref[...])
    if final_norm:
        h = _rms(h, fg_ref[...])
    o_ref[...] = h


FFN_TM = 512


def _ffn(h, g, w_in_all, w_out_all, layer, which, final_g, *, final_norm, mix=None):
    m, d = h.shape
    d_ff = w_out_all.shape[2]
    tm = min(FFN_TM, m)
    rows = lambda width: pl.BlockSpec((tm, width), lambda i: (i, 0))
    resident = lambda shape: pl.BlockSpec(shape, lambda i: (0, 0), pipeline_mode=pl.Buffered(1))
    stacked = lambda shape: pl.BlockSpec((None, None) + shape, lambda i: (layer, which, 0, 0),
                                         pipeline_mode=pl.Buffered(1))
    operands, in_specs = [h], [rows(d)]
    if mix is not None:
        a, b, w_mix = mix
        operands += [a, b, w_mix]
        in_specs += [rows(a.shape[1]), rows(b.shape[1]), resident(w_mix.shape)]
    operands += [g.reshape(1, d), w_in_all, w_out_all, final_g.reshape(1, d)]
    in_specs += [resident((1, d)), stacked((d, 2 * d_ff)), stacked((d_ff, d)), resident((1, d))]
    return pl.pallas_call(
        functools.partial(_ffn_kernel, final_norm=final_norm, with_mix=mix is not None),
        grid=(m // tm,),
        in_specs=in_specs,
        out_specs=rows(d),
        out_shape=jax.ShapeDtypeStruct((m, d), F32),
        compiler_params=_params(("parallel",)),
        name="ffn",
    )(*operands)


def _proj_in_kernel(x_ref, g_ref, w_ref, o_ref):
    o_ref[...] = _dot(_rms(x_ref[...], g_ref[...]).astype(BF16), w_ref[...]).astype(o_ref.dtype)


PROJ_TM = 512


def _proj_in(h, g, w):
    m, d = h.shape
    n_out = w.shape[1]
    tm = min(PROJ_TM, m)
    resident = lambda shape: pl.BlockSpec(shape, lambda i: (0, 0), pipeline_mode=pl.Buffered(1))
    return pl.pallas_call(
        _proj_in_kernel,
        grid=(m // tm,),
        in_specs=[pl.BlockSpec((tm, d), lambda i: (i, 0)), resident((1, d)), resident((d, n_out))],
        out_specs=pl.BlockSpec((tm, n_out), lambda i: (i, 0)),
        out_shape=jax.ShapeDtypeStruct((m, n_out), BF16),
        compiler_params=_params(("parallel",)),
        name="proj_in",
    )(h, g.reshape(1, d), w)


def _proj_rec_kernel(x_ref, g_ref, w_ref, cw_ref, o_ref, f_ref, od_ref, tail_scr, *, seq_tiles):
    y = _dot(_rms(x_ref[...], g_ref[...]).astype(BF16), w_ref[...])
    hg = 4 * HG_WIDTH
    o_ref[...] = y[:, :hg].astype(o_ref.dtype)
    f_ref[...] = y[:, HG_WIDTH:2 * HG_WIDTH]
    gate_b = y[:, hg:hg + CONV_WIDTH]
    x = y[:, hg + CONV_WIDTH:hg + 2 * CONV_WIDTH] * y[:, hg + 2 * CONV_WIDTH:]
    first = pl.program_id(0) % seq_tiles == 0
    tail = jnp.where(first, 0.0, tail_scr[...])
    row = lax.broadcasted_iota(jnp.int32, x.shape, 0)
    t1 = tail[SUBLANES - 1:SUBLANES, :]
    t2 = tail[SUBLANES - 2:SUBLANES - 1, :]
    x1 = jnp.where(row == 0, t1, pltpu.roll(x, 1, 0))
    x2 = jnp.where(row == 0, t2, jnp.where(row == 1, t1, pltpu.roll(x, 2, 0)))
    conv = cw_ref[0:1, :] * x2 + cw_ref[1:2, :] * x1 + cw_ref[2:3, :] * x
    od_ref[...] = (gate_b * conv).astype(od_ref.dtype)
    tail_scr[...] = x[x.shape[0] - SUBLANES:, :]


def _proj_rec(h, g, w, conv_w, seq_len):
    m, d = h.shape
    tm = min(PROJ_TM, seq_len)
    assert CONV_K - 1 <= SUBLANES and seq_len % tm == 0
    resident = lambda shape: pl.BlockSpec(shape, lambda i: (0, 0), pipeline_mode=pl.Buffered(1))
    rows = lambda width: pl.BlockSpec((tm, width), lambda i: (i, 0))
    return pl.pallas_call(
        functools.partial(_proj_rec_kernel, seq_tiles=seq_len // tm),
        grid=(m // tm,),
        in_specs=[rows(d), resident((1, d)), resident((d, REC_IN)), resident((CONV_K, CONV_WIDTH))],
        out_specs=[rows(4 * HG_WIDTH), rows(HG_WIDTH), rows(CONV_WIDTH)],
        out_shape=[jax.ShapeDtypeStruct((m, 4 * HG_WIDTH), BF16), jax.ShapeDtypeStruct((m, HG_WIDTH), F32),
                   jax.ShapeDtypeStruct((m, CONV_WIDTH), BF16)],
        scratch_shapes=[pltpu.VMEM((SUBLANES, CONV_WIDTH), F32)],
        compiler_params=_params(("arbitrary",)),
        name="proj_rec",
    )(h, g.reshape(1, d), w, conv_w)


def _swa_kernel(sink_ref, q_ref, k_ref, v_ref, o_ref, k_swapped, v_swapped):
    nb = q_ref.shape[1] // BLK
    half = HEAD_DIM
    k_swapped[...] = pltpu.roll(k_ref[0].astype(F32), half, 1).astype(BF16)
    v_swapped[...] = pltpu.roll(v_ref[0].astype(F32), half, 1).astype(BF16)
    qi = lax.broadcasted_iota(jnp.int32, (BLK, 2 * BLK), 0)
    kj = lax.broadcasted_iota(jnp.int32, (BLK, 2 * BLK), 1)
    dist = qi + BLK - kj
    in_window = (dist >= 0) & (dist < WINDOW)
    dist_f = dist.astype(F32)
    lane = lax.broadcasted_iota(jnp.int32, (BLK, BLK), 1)
    scale = HEAD_DIM ** -0.5

    def q_block(n, _):
        cur = pl.ds(pl.multiple_of(n * BLK, BLK), BLK)
        prev = pl.ds(pl.multiple_of(jnp.maximum(n - 1, 0) * BLK, BLK), BLK)
        valid = in_window & (n * BLK - BLK + kj >= 0)
        penalty = jnp.where(valid, dist_f, jnp.inf)
        band = lambda ref: jnp.concatenate([ref[prev, :], ref[cur, :]], axis=0)
        k_by_half = (band(k_ref.at[0]), band(k_swapped))
        v_by_half = (band(v_ref.at[0]), band(v_swapped))
        heads = range(SWA_HEADS)
        kv_half = [(h // SWA_GROUP) ^ (h % 2) for h in heads]
        scores = []
        for h in heads:
            q_col = q_ref[0, cur, (h // 2) * BLK:(h // 2 + 1) * BLK].astype(F32) * scale
            in_half = (lane >= (h % 2) * half) & (lane < (h % 2 + 1) * half)
            qm = jnp.where(in_half, q_col, 0.0).astype(BF16)
            slope = 2.0 ** (-8.0 * (h + 1) / SWA_HEADS)
            scores.append(_dot_nt(qm, k_by_half[kv_half[h]]) - slope * penalty)
        row_max = [jnp.maximum(jnp.max(scores[h], axis=-1, keepdims=True), sink_ref[h]) for h in heads]
        probs = [jnp.exp(scores[h] - row_max[h]) for h in heads]
        num = [_dot(probs[h].astype(BF16), v_by_half[kv_half[h]]) for h in heads]
        den = [jnp.sum(probs[h], axis=-1, keepdims=True) + jnp.exp(sink_ref[h] - row_max[h]) for h in heads]
        out = [num[h] / den[h] for h in heads]
        for col in range(SWA_Q // BLK):
            o_ref[0, cur, col * BLK:(col + 1) * BLK] = jnp.where(
                lane < half, out[2 * col], out[2 * col + 1]).astype(o_ref.dtype)
        return 0

    lax.fori_loop(0, nb, q_block, 0, unroll=SWA_UNROLL)


def _swa(proj, sinks):
    b, s, _ = proj.shape
    assert SWA_KV == BLK and SWA_KV_HEADS == 2
    q_col = 3 * SB_W // SWA_Q
    k_col = (3 * SB_W + SWA_Q) // SWA_KV
    v_col = k_col + 1
    return pl.pallas_call(
        _swa_kernel,
        grid=(b,),
        in_specs=[
            pl.BlockSpec(memory_space=pltpu.SMEM),
            pl.BlockSpec((1, s, SWA_Q), lambda i: (i, 0, q_col)),
            pl.BlockSpec((1, s, SWA_KV), lambda i: (i, 0, k_col)),
            pl.BlockSpec((1, s, SWA_KV), lambda i: (i, 0, v_col)),
        ],
        out_specs=pl.BlockSpec((1, s, SWA_Q), lambda i: (i, 0, 0)),
        out_shape=jax.ShapeDtypeStruct((b, s, SWA_Q), BF16),
        scratch_shapes=[pltpu.VMEM((s, SWA_KV), BF16), pltpu.VMEM((s, SWA_KV), BF16)],
        compiler_params=_params(("parallel",)),
        name="swa",
    )(sinks, proj, proj, proj)


SB_TQ = 256
SB_KPI = 2
LOG2E = math.log2(math.e)
SB_DEAD = -256.0


SB_PAIRS = 4


def _sb_kernel(q_ref, k_ref, v_ref, o_ref, kk_scr, vv_scr, carry_scr, acc_scr):
    s = q_ref.shape[1]
    nt = s // SB_TQ
    nb = s // BLK
    kpt = SB_TQ // BLK
    lane = lax.broadcasted_iota(jnp.int32, (BLK, BLK), 1)
    key_pos = lax.broadcasted_iota(jnp.int32, (SB_TQ, 2 * BLK), 1) & (BLK - 1)
    q_pos = lax.broadcasted_iota(jnp.int32, (SB_TQ, 2 * BLK), 0)
    jj = lax.broadcasted_iota(jnp.int32, (2 * BLK, 2 * BLK), 0) & (BLK - 1)
    ss = lax.broadcasted_iota(jnp.int32, (2 * BLK, 2 * BLK), 1)
    suffix_mat = jnp.where((jj > ss) | (ss >= BLK), 1.0, 0.0).astype(BF16)
    scale = HEAD_DIM ** -0.5

    def prep(j, _):
        rows = pl.ds(pl.multiple_of(j * BLK, BLK), BLK)
        for p in range(SB_PAIRS):
            kb = k_ref[0, rows, p * BLK:(p + 1) * BLK]
            vb = v_ref[0, rows, p * BLK:(p + 1) * BLK]
            kk_scr[p, j] = jnp.concatenate([jnp.where(lane < HEAD_DIM, kb, 0.0),
                                            jnp.where(lane >= HEAD_DIM, kb, 0.0)], axis=0).astype(BF16)
            vv_scr[p, j] = jnp.concatenate([jnp.where(lane < HEAD_DIM, vb, 0.0),
                                            jnp.where(lane >= HEAD_DIM, vb, 0.0)], axis=0).astype(BF16)
        return 0

    lax.fori_loop(0, nb, prep, 0)

    def key_blocks(qn, t, js, masked):
        first_row = [(len(js) - 1 - d) * BLK if masked else 0 for d in range(len(js))]
        chains = [(p, d) for p in range(SB_PAIRS) for d in range(len(js))]
        causal = [((js[d] * BLK + key_pos) < (t * SB_TQ + q_pos))[first_row[d]:] if masked else None
                  for d in range(len(js))]
        nz = {(p, d): _dot_nt(qn[p][first_row[d]:], kk_scr[p, js[d]]) for p, d in chains}
        log_beta, lhs = {}, {}
        for c in chains:
            log_keep = jnp.minimum(nz[c], 0.0) - jnp.log2(1.0 + jnp.exp2(-jnp.abs(nz[c])))
            log_beta[c] = log_keep - nz[c]
            if masked:
                log_keep = jnp.where(causal[c[1]], log_keep, 0.0)
            hi = log_keep.astype(BF16)
            lo = (log_keep - hi.astype(F32)).astype(BF16)
            lhs[c] = [jnp.concatenate([hi[:, h * BLK:(h + 1) * BLK], lo[:, h * BLK:(h + 1) * BLK]], axis=1)
                      for h in range(2)]
        cs = {c: [_dot(lhs[c][h], suffix_mat) for h in range(2)] for c in chains}
        w = {}
        for p in range(SB_PAIRS):
            carry = carry_scr[p]
            for d in range(len(js)):
                c, r0 = (p, d), first_row[d]
                between = jnp.concatenate([cs[c][0][:, :BLK], cs[c][1][:, :BLK]], axis=1)
                wc = jnp.exp2(log_beta[c] + between + carry[r0:])
                if masked:
                    wc = jnp.where(causal[d], wc, 0.0)
                w[c] = wc.astype(BF16)
                below = carry[r0:] + jnp.concatenate([cs[c][0][:, BLK:], cs[c][1][:, BLK:]], axis=1)
                carry = below if r0 == 0 else jnp.concatenate([carry[:r0], below], axis=0)
            carry_scr[p] = carry
        for p in range(SB_PAIRS):
            acc = acc_scr[p]
            for d, j in enumerate(js):
                r0 = first_row[d]
                below = acc[r0:] + _dot(w[(p, d)], vv_scr[p, j])
                acc = below if r0 == 0 else jnp.concatenate([acc[:r0], below], axis=0)
            acc_scr[p] = acc

    def q_tile(t, _):
        rows = pl.ds(pl.multiple_of(t * SB_TQ, SB_TQ), SB_TQ)
        qn = [(q_ref[0, rows, p * BLK:(p + 1) * BLK].astype(F32) * (-scale * LOG2E)).astype(BF16)
              for p in range(SB_PAIRS)]
        carry_scr[...] = jnp.zeros_like(carry_scr)
        acc_scr[...] = jnp.zeros_like(acc_scr)
        key_blocks(qn, t, [kpt * t + kpt - 1 - d for d in range(kpt)], True)

        def live(state):
            return (state[0] < kpt * t // SB_KPI) & (state[1] > SB_DEAD)

        def kv_step(state):
            i = state[0]
            key_blocks(qn, t, [kpt * t - 1 - i * SB_KPI - d for d in range(SB_KPI)], False)
            return i + 1, jnp.max(carry_scr[...])

        lax.while_loop(live, kv_step, (jnp.int32(0), jnp.float32(0.0)))
        for p in range(SB_PAIRS):
            o_ref[0, rows, p * BLK:(p + 1) * BLK] = acc_scr[p].astype(o_ref.dtype)
        return 0

    lax.fori_loop(0, nt, q_tile, 0)


def _sb(proj):
    b, s, _ = proj.shape
    assert (SB_TQ // BLK) % SB_KPI == 0
    width = SB_PAIRS * BLK
    q_col = 0
    k_col = SB_W // width
    v_col = 2 * SB_W // width
    return pl.pallas_call(
        _sb_kernel,
        grid=(b, SB_W // width),
        in_specs=[
            pl.BlockSpec((1, s, width), lambda i, p: (i, 0, q_col + p)),
            pl.BlockSpec((1, s, width), lambda i, p: (i, 0, k_col + p)),
            pl.BlockSpec((1, s, width), lambda i, p: (i, 0, v_col + p)),
        ],
        out_specs=pl.BlockSpec((1, s, width), lambda i, p: (i, 0, p)),
        out_shape=jax.ShapeDtypeStruct((b, s, SB_W), BF16),
        scratch_shapes=[pltpu.VMEM((SB_PAIRS, s // BLK, 2 * BLK, BLK), BF16),
                        pltpu.VMEM((SB_PAIRS, s // BLK, 2 * BLK, BLK), BF16),
                        pltpu.VMEM((SB_PAIRS, SB_TQ, 2 * BLK), F32), pltpu.VMEM((SB_PAIRS, SB_TQ, BLK), F32)],
        compiler_params=_params(("parallel", "parallel")),
        name="stick_breaking",
    )(proj, proj, proj)


def _hgrn_kernel(layer, lbl_ref, ng_ref, q_ref, f_ref, i_ref, gate_ref, o_ref, state_scr):
    nc = q_ref.shape[1] // HG_CHUNK
    logits = lbl_ref[...]
    e = jnp.exp(logits - jnp.max(logits, axis=0, keepdims=True))
    sm = e / jnp.sum(e, axis=0, keepdims=True)
    lb = jnp.sum(sm[1:layer + 1], axis=0, keepdims=True)
    ti = lax.broadcasted_iota(jnp.int32, (HG_CHUNK, HG_CHUNK), 0)
    si = lax.broadcasted_iota(jnp.int32, (HG_CHUNK, HG_CHUNK), 1)
    cumsum_mat = jnp.where(si <= ti, 1.0, 0.0).astype(BF16)
    row8 = lax.broadcasted_iota(jnp.int32, (SUBLANES, 1), 0)
    n_sub = HG_CHUNK // HG_SUB
    state_scr[...] = jnp.zeros_like(state_scr)

    def pair_rows(acc, qt, gt, ks, gs, vs, s):
        decay = jnp.exp2(gt - gs[s:s + 1, :])
        col = jnp.sum(qt * (ks[s:s + 1, :] * decay), axis=-1, keepdims=True)
        col = jnp.where(row8 >= s, col, 0.0)
        return acc + col * vs[s:s + 1, :]

    def chunk(c, _, single_reference):
        rows = pl.ds(pl.multiple_of(c * HG_CHUNK, HG_CHUNK), HG_CHUNK)
        z = f_ref[0, rows, :]
        qx = q_ref[0, rows, :].astype(F32)
        v_all = i_ref[0, rows, :].astype(F32)
        gate = gate_ref[0, rows, :].astype(F32)
        sig = _sigmoid(z)
        q_all = qx * _sigmoid(qx)
        k_all = (1.0 - lb) * (1.0 - sig)
        log2_f = jnp.log2(lb + (1.0 - lb) * sig)
        g_all = _split_dot_left(cumsum_mat, log2_f)
        g_last_all = g_all[HG_CHUNK - 1:HG_CHUNK, :]
        gated = gate * _sigmoid(gate)
        heads = range(HG_HEADS)
        cols = [slice(h * HG_DK, (h + 1) * HG_DK) for h in heads]
        v_bf = v_all.astype(BF16)
        state = [state_scr[h] for h in heads]
        if single_reference:
            r = 0.5 * g_last_all
            half = jnp.exp2(r)
            qs = q_all * jnp.exp2(g_all - r)
            ks = k_all * jnp.exp2(r - g_all)
            qg_all, kd_all = (qs * half).astype(BF16), (ks * half).astype(BF16)
            qs, ks = qs.astype(BF16), ks.astype(BF16)
        else:
            qg_all = (q_all * jnp.exp2(g_all)).astype(BF16)
            kd_all = (k_all * jnp.exp2(g_last_all - g_all)).astype(BF16)
        o_inter = [_dot_nt(qg_all[:, cols[h]], state[h].astype(BF16)) for h in heads]
        new_state = [state[h] * jnp.exp2(g_last_all[:, cols[h]]) + _dot_tn(v_bf[:, cols[h]], kd_all[:, cols[h]])
                     for h in heads]
        if single_reference:
            sc = [jnp.where(si <= ti, _dot_nt(qs[:, cols[h]], ks[:, cols[h]]), 0.0).astype(BF16) for h in heads]
            intra = [_dot(sc[h], v_bf[:, cols[h]]) for h in heads]
            for h in heads:
                state_scr[h] = new_state[h]
                o_ref[0, rows, cols[h]] = (_rms(o_inter[h] + intra[h], ng_ref[...]) * gated[:, cols[h]]
                                           ).astype(o_ref.dtype)
            return 0
        kt_all, qt_all = [], []
        for j in range(n_sub - 1):
            lo, hi = j * HG_SUB, (j + 1) * HG_SUB
            r = g_all[hi - 1:hi, :]
            kt_all.append((k_all[lo:hi] * jnp.exp2(r - g_all[lo:hi])).astype(BF16))
            qt_all.append((q_all[hi:] * jnp.exp2(g_all[hi:] - r)).astype(BF16))
        q_low, k_up, v_up = [], [], []
        for i in range(n_sub):
            lo, mid, hi = i * HG_SUB, i * HG_SUB + SUBLANES, (i + 1) * HG_SUB
            r = g_all[mid - 1:mid, :]
            q_low.append(q_all[mid:hi] * jnp.exp2(g_all[mid:hi] - r))
            k_up.append(k_all[lo:mid] * jnp.exp2(r - g_all[lo:mid]))
            v_up.append(v_all[lo:mid])
        q_low = jnp.concatenate(q_low, axis=0).astype(BF16)
        k_up = jnp.concatenate(k_up, axis=0).astype(BF16)
        v_up = jnp.concatenate(v_up, axis=0).astype(BF16)
        same_sub = (lax.broadcasted_iota(jnp.int32, (n_sub * SUBLANES, n_sub * SUBLANES), 0) // SUBLANES
                    == lax.broadcasted_iota(jnp.int32, (n_sub * SUBLANES, n_sub * SUBLANES), 1) // SUBLANES)
        sc = [[_dot_nt(qt_all[j][:, cols[h]], kt_all[j][:, cols[h]]).astype(BF16)
               for j in range(n_sub - 1)] for h in heads]
        contrib = [[_dot(sc[h][j], v_bf[j * HG_SUB:(j + 1) * HG_SUB, cols[h]])
                    for j in range(n_sub - 1)] for h in heads]
        sc_low = [jnp.where(same_sub, _dot_nt(q_low[:, cols[h]], k_up[:, cols[h]]), 0.0).astype(BF16) for h in heads]
        contrib_low = [_dot(sc_low[h], v_up[:, cols[h]]) for h in heads]
        tiles = [[None] * (HG_CHUNK // SUBLANES) for _ in heads]
        for h in heads:
            for i in range(n_sub):
                part = jnp.zeros((HG_SUB, HG_DK), F32)
                for j in range(i):
                    part = part + contrib[h][j][(i - j - 1) * HG_SUB:(i - j) * HG_SUB]
                tiles[h][2 * i] = part[:SUBLANES]
                tiles[h][2 * i + 1] = part[SUBLANES:] + contrib_low[h][i * SUBLANES:(i + 1) * SUBLANES]
        for i in range(2 * n_sub):
            a = slice(i * SUBLANES, (i + 1) * SUBLANES)
            for s in range(SUBLANES):
                for h in heads:
                    g, q, k, v = g_all[:, cols[h]], q_all[:, cols[h]], k_all[:, cols[h]], v_all[:, cols[h]]
                    tiles[h][i] = pair_rows(tiles[h][i], q[a], g[a], k[a], g[a], v[a], s)
        for h in heads:
            o = o_inter[h] + jnp.concatenate(tiles[h], axis=0)
            state_scr[h] = new_state[h]
            o_ref[0, rows, cols[h]] = (_rms(o, ng_ref[...]) * gated[:, cols[h]]).astype(o_ref.dtype)
        return 0

    def chunk_decay(c, worst):
        rows = pl.ds(pl.multiple_of(c * HG_CHUNK, HG_CHUNK), HG_CHUNK)
        log2_f = jnp.log2(lb + (1.0 - lb) * _sigmoid(f_ref[0, rows, :]))
        return jnp.minimum(worst, jnp.sum(log2_f, axis=0, keepdims=True))

    worst = jnp.min(lax.fori_loop(0, nc, chunk_decay, jnp.zeros((1, HG_WIDTH), F32), unroll=HG_UNROLL))
    gentle = worst >= -HG_SAFE_DECAY

    @pl.when(gentle)
    def _():
        lax.fori_loop(0, nc, functools.partial(chunk, single_reference=True), 0, unroll=HG_UNROLL_SINGLE)

    @pl.when(jnp.logical_not(gentle))
    def _():
        lax.fori_loop(0, nc, functools.partial(chunk, single_reference=False), 0, unroll=HG_UNROLL)


def _hgrn(proj, f_logit, lb_logits, norm_g, layer):
    b, s, _ = proj.shape
    depth = lb_logits.shape[0]
    assert HG_SUB == 2 * SUBLANES
    blk = lambda off: pl.BlockSpec((1, s, HG_WIDTH), lambda i: (i, 0, off))
    return pl.pallas_call(
        functools.partial(_hgrn_kernel, layer),
        grid=(b,),
        in_specs=[
            pl.BlockSpec((depth, HG_WIDTH), lambda i: (0, 0)),
            pl.BlockSpec((1, HG_DK), lambda i: (0, 0)),
            blk(0), blk(0), blk(2), blk(3),
        ],
        out_specs=pl.BlockSpec((1, s, HG_WIDTH), lambda i: (i, 0, 0)),
        out_shape=jax.ShapeDtypeStruct((b, s, HG_WIDTH), BF16),
        scratch_shapes=[pltpu.VMEM((HG_HEADS, HG_DK, HG_DK), F32)],
        compiler_params=_params(("parallel",)),
        name="hgrn2",
    )(lb_logits, norm_g.reshape(1, HG_DK), proj, f_logit, proj, proj)


def _attn_columns(w_in):
    split = SWA_Q + 2 * SWA_KV
    return jnp.concatenate([w_in[:, split:], w_in[:, :split]], axis=1)


def kernel(x, norm_g, ffn_w_in, ffn_w_out, attn_w_in, attn_sinks, attn_w_out, rec_w_in,
           hgrn_lb_logits, hgrn_norm_g, conv_w, rec_w_out, final_g):
    b, s, d = x.shape
    depth = norm_g.shape[0]
    m = b * s
    h = x.reshape(m, d)
    w_in_bf, w_out_bf = ffn_w_in.astype(BF16), ffn_w_out.astype(BF16)
    for layer in range(depth):
        h = _ffn(h, norm_g[layer, 0], w_in_bf, w_out_bf, layer, 0, final_g, final_norm=False)
        if layer % 2 == 0:
            e = layer // 2
            proj = _proj_in(h, norm_g[layer, 1], _attn_columns(attn_w_in[e]).astype(BF16)).reshape(b, s, ATTN_IN)
            o_a = _swa(proj, attn_sinks[e])
            o_b = _sb(proj)
            mix = (o_a.reshape(m, SWA_Q), o_b.reshape(m, SB_W), attn_w_out[e].astype(BF16))
        else:
            o = layer // 2
            proj, f_logit, o_d = _proj_rec(h, norm_g[layer, 1], rec_w_in[o].astype(BF16), conv_w[o], s)
            o_c = _hgrn(proj.reshape(b, s, 4 * HG_WIDTH), f_logit.reshape(b, s, HG_WIDTH), hgrn_lb_logits,
                        hgrn_norm_g[o], layer)
            mix = (o_c.reshape(m, HG_WIDTH), o_d, rec_w_out[o].astype(BF16))
        h = _ffn(h, norm_g[layer, 2], w_in_bf, w_out_bf, layer, 1, final_g,
                 final_norm=(layer == depth - 1), mix=mix)
    return h.reshape(b, s, d)
```

```python
import functools
import math

import jax
import jax.numpy as jnp
from jax import lax
from jax.experimental import pallas as pl
from jax.experimental.pallas import tpu as pltpu

F32 = jnp.float32
BF16 = jnp.bfloat16

D_MODEL = 1024
HEAD_DIM = 64
SWA_HEADS = 8
SWA_KV_HEADS = 2
SWA_GROUP = SWA_HEADS // SWA_KV_HEADS
WINDOW = 128
BLK = 128
SUBLANES = 8
SB_HEADS = 8
HG_WIDTH = 512
HG_DK = 128
HG_HEADS = HG_WIDTH // HG_DK
HG_CHUNK = 64
HG_SUB = 16
HG_UNROLL = 4
HG_GROUP = 8
HG_SAFE_DECAY = 200.0
SWA_UNROLL = 4
CONV_WIDTH = 512
CONV_K = 3
D_FF = 2816
EPS = 1e-6

SWA_Q = SWA_HEADS * HEAD_DIM
SWA_KV = SWA_KV_HEADS * HEAD_DIM
SB_W = SB_HEADS * HEAD_DIM
ATTN_IN = SWA_Q + 2 * SWA_KV + 3 * SB_W
REC_IN = 4 * HG_WIDTH + 3 * CONV_WIDTH

VMEM_LIMIT_BYTES = 56 * 1024 * 1024


def _params(semantics):
    return pltpu.CompilerParams(dimension_semantics=semantics,
                                vmem_limit_bytes=VMEM_LIMIT_BYTES)


def _rms(x, g):
    return x * lax.rsqrt(jnp.mean(x * x, axis=-1, keepdims=True) + EPS) * g


def _sigmoid(x):
    return 0.5 * (1.0 + jnp.tanh(0.5 * x))


def _dot(a, b):
    return jnp.dot(a, b, preferred_element_type=F32)


def _dot_nt(a, b):
    return lax.dot_general(a, b, (((1,), (1,)), ((), ())), preferred_element_type=F32)


def _dot_tn(a, b):
    return lax.dot_general(a, b, (((0,), (0,)), ((), ())), preferred_element_type=F32)


def _split_dot(x, m01):
    hi = x.astype(BF16)
    lo = (x - hi.astype(F32)).astype(BF16)
    return _dot(hi, m01) + _dot(lo, m01)


def _split_dot_left(m01, x):
    hi = x.astype(BF16)
    lo = (x - hi.astype(F32)).astype(BF16)
    return _dot(m01, hi) + _dot(m01, lo)


def _ffn_kernel(*refs, final_norm, with_mix):
    if with_mix:
        x_ref, a_ref, b_ref, wm_ref, g_ref, wi_ref, wo_ref, fg_ref, o_ref = refs
    else:
        x_ref, g_ref, wi_ref, wo_ref, fg_ref, o_ref = refs
    d_ff = wo_ref.shape[0]
    x = x_ref[...]
    if with_mix:
        ka = a_ref.shape[1]
        x = x + _dot(a_ref[...], wm_ref[:ka, :]) + _dot(b_ref[...], wm_ref[ka:, :])
    n = _rms(x, g_ref[...]).astype(BF16)
    gate = _dot(n, wi_ref[:, :d_ff])
    up = _dot(n, wi_ref[:, d_ff:])
    act = (gate * jax.nn.sigmoid(gate) * up).astype(BF16)
    h = x + 0.5 * _dot(act, wo_ref[...])
    if final_norm:
        h = _rms(h, fg_ref[...])
    o_ref[...] = h


FFN_TM = 512


def _ffn(h, g, w_in_all, w_out_all, layer, which, final_g, *, final_norm, mix=None):
    m, d = h.shape
    d_ff = w_out_all.shape[2]
    tm = min(FFN_TM, m)
    rows = lambda width: pl.BlockSpec((tm, width), lambda i: (i, 0))
    resident = lambda shape: pl.BlockSpec(shape, lambda i: (0, 0), pipeline_mode=pl.Buffered(1))
    stacked = lambda shape: pl.BlockSpec((None, None) + shape, lambda i: (layer, which, 0, 0),
                                         pipeline_mode=pl.Buffered(1))
    operands, in_specs = [h], [rows(d)]
    if mix is not None:
        a, b, w_mix = mix
        operands += [a, b, w_mix]
        in_specs += [rows(a.shape[1]), rows(b.shape[1]), resident(w_mix.shape)]
    operands += [g.reshape(1, d), w_in_all, w_out_all, final_g.reshape(1, d)]
    in_specs += [resident((1, d)), stacked((d, 2 * d_ff)), stacked((d_ff, d)), resident((1, d))]
    return pl.pallas_call(
        functools.partial(_ffn_kernel, final_norm=final_norm, with_mix=mix is not None),
        grid=(m // tm,),
        in_specs=in_specs,
        out_specs=rows(d),
        out_shape=jax.ShapeDtypeStruct((m, d), F32),
        compiler_params=_params(("parallel",)),
        name="ffn",
    )(*operands)


def _proj_in_kernel(x_ref, g_ref, w_ref, o_ref):
    o_ref[...] = _dot(_rms(x_ref[...], g_ref[...]).astype(BF16), w_ref[...]).astype(o_ref.dtype)


PROJ_TM = 512


def _proj_in(h, g, w):
    m, d = h.shape
    n_out = w.shape[1]
    tm = min(PROJ_TM, m)
    resident = lambda shape: pl.BlockSpec(shape, lambda i: (0, 0), pipeline_mode=pl.Buffered(1))
    return pl.pallas_call(
        _proj_in_kernel,
        grid=(m // tm,),
        in_specs=[pl.BlockSpec((tm, d), lambda i: (i, 0)), resident((1, d)), resident((d, n_out))],
        out_specs=pl.BlockSpec((tm, n_out), lambda i: (i, 0)),
        out_shape=jax.ShapeDtypeStruct((m, n_out), BF16),
        compiler_params=_params(("parallel",)),
        name="proj_in",
    )(h, g.reshape(1, d), w)


def _proj_rec_kernel(x_ref, g_ref, w_ref, cw_ref, o_ref, f_ref, od_ref, tail_scr, *, seq_tiles):
    y = _dot(_rms(x_ref[...], g_ref[...]).astype(BF16), w_ref[...])
    hg = 4 * HG_WIDTH
    o_ref[...] = y[:, :hg].astype(o_ref.dtype)
    f_ref[...] = y[:, HG_WIDTH:2 * HG_WIDTH]
    gate_b = y[:, hg:hg + CONV_WIDTH]
    x = y[:, hg + CONV_WIDTH:hg + 2 * CONV_WIDTH] * y[:, hg + 2 * CONV_WIDTH:]
    first = pl.program_id(0) % seq_tiles == 0
    tail = jnp.where(first, 0.0, tail_scr[...])
    row = lax.broadcasted_iota(jnp.int32, x.shape, 0)
    t1 = tail[SUBLANES - 1:SUBLANES, :]
    t2 = tail[SUBLANES - 2:SUBLANES - 1, :]
    x1 = jnp.where(row == 0, t1, pltpu.roll(x, 1, 0))
    x2 = jnp.where(row == 0, t2, jnp.where(row == 1, t1, pltpu.roll(x, 2, 0)))
    conv = cw_ref[0:1, :] * x2 + cw_ref[1:2, :] * x1 + cw_ref[2:3, :] * x
    od_ref[...] = (gate_b * conv).astype(od_ref.dtype)
    tail_scr[...] = x[x.shape[0] - SUBLANES:, :]


def _proj_rec(h, g, w, conv_w, seq_len):
    m, d = h.shape
    tm = min(PROJ_TM, seq_len)
    assert CONV_K - 1 <= SUBLANES and seq_len % tm == 0
    resident = lambda shape: pl.BlockSpec(shape, lambda i: (0, 0), pipeline_mode=pl.Buffered(1))
    rows = lambda width: pl.BlockSpec((tm, width), lambda i: (i, 0))
    return pl.pallas_call(
        functools.partial(_proj_rec_kernel, seq_tiles=seq_len // tm),
        grid=(m // tm,),
        in_specs=[rows(d), resident((1, d)), resident((d, REC_IN)), resident((CONV_K, CONV_WIDTH))],
        out_specs=[rows(4 * HG_WIDTH), rows(HG_WIDTH), rows(CONV_WIDTH)],
        out_shape=[jax.ShapeDtypeStruct((m, 4 * HG_WIDTH), BF16), jax.ShapeDtypeStruct((m, HG_WIDTH), F32),
                   jax.ShapeDtypeStruct((m, CONV_WIDTH), BF16)],
        scratch_shapes=[pltpu.VMEM((SUBLANES, CONV_WIDTH), F32)],
        compiler_params=_params(("arbitrary",)),
        name="proj_rec",
    )(h, g.reshape(1, d), w, conv_w)


def _swa_kernel(sink_ref, q_ref, k_ref, v_ref, o_ref, k_swapped, v_swapped):
    nb = q_ref.shape[1] // BLK
    half = HEAD_DIM
    k_swapped[...] = pltpu.roll(k_ref[0].astype(F32), half, 1).astype(BF16)
    v_swapped[...] = pltpu.roll(v_ref[0].astype(F32), half, 1).astype(BF16)
    qi = lax.broadcasted_iota(jnp.int32, (BLK, 2 * BLK), 0)
    kj = lax.broadcasted_iota(jnp.int32, (BLK, 2 * BLK), 1)
    dist = qi + BLK - kj
    in_window = (dist >= 0) & (dist < WINDOW)
    dist_f = dist.astype(F32)
    lane = lax.broadcasted_iota(jnp.int32, (BLK, BLK), 1)
    scale = HEAD_DIM ** -0.5

    def q_block(n, _):
        cur = pl.ds(pl.multiple_of(n * BLK, BLK), BLK)
        prev = pl.ds(pl.multiple_of(jnp.maximum(n - 1, 0) * BLK, BLK), BLK)
        valid = in_window & (n * BLK - BLK + kj >= 0)
        penalty = jnp.where(valid, dist_f, jnp.inf)
        band = lambda ref: jnp.concatenate([ref[prev, :], ref[cur, :]], axis=0)
        k_by_half = (band(k_ref.at[0]), band(k_swapped))
        v_by_half = (band(v_ref.at[0]), band(v_swapped))
        heads = range(SWA_HEADS)
        kv_half = [(h // SWA_GROUP) ^ (h % 2) for h in heads]
        scores = []
        for h in heads:
            q_col = q_ref[0, cur, (h // 2) * BLK:(h // 2 + 1) * BLK].astype(F32) * scale
            in_half = (lane >= (h % 2) * half) & (lane < (h % 2 + 1) * half)
            qm = jnp.where(in_half, q_col, 0.0).astype(BF16)
            slope = 2.0 ** (-8.0 * (h + 1) / SWA_HEADS)
            scores.append(_dot_nt(qm, k_by_half[kv_half[h]]) - slope * penalty)
        row_max = [jnp.maximum(jnp.max(scores[h], axis=-1, keepdims=True), sink_ref[h]) for h in heads]
        probs = [jnp.exp(scores[h] - row_max[h]) for h in heads]
        num = [_dot(probs[h].astype(BF16), v_by_half[kv_half[h]]) for h in heads]
        den = [jnp.sum(probs[h], axis=-1, keepdims=True) + jnp.exp(sink_ref[h] - row_max[h]) for h in heads]
        out = [num[h] / den[h] for h in heads]
        for col in range(SWA_Q // BLK):
            o_ref[0, cur, col * BLK:(col + 1) * BLK] = jnp.where(
                lane < half, out[2 * col], out[2 * col + 1]).astype(o_ref.dtype)
        return 0

    lax.fori_loop(0, nb, q_block, 0, unroll=SWA_UNROLL)


def _swa(proj, sinks):
    b, s, _ = proj.shape
    assert SWA_KV == BLK and SWA_KV_HEADS == 2
    q_col = 3 * SB_W // SWA_Q
    k_col = (3 * SB_W + SWA_Q) // SWA_KV
    v_col = k_col + 1
    return pl.pallas_call(
        _swa_kernel,
        grid=(b,),
        in_specs=[
            pl.BlockSpec(memory_space=pltpu.SMEM),
            pl.BlockSpec((1, s, SWA_Q), lambda i: (i, 0, q_col)),
            pl.BlockSpec((1, s, SWA_KV), lambda i: (i, 0, k_col)),
            pl.BlockSpec((1, s, SWA_KV), lambda i: (i, 0, v_col)),
        ],
        out_specs=pl.BlockSpec((1, s, SWA_Q), lambda i: (i, 0, 0)),
        out_shape=jax.ShapeDtypeStruct((b, s, SWA_Q), BF16),
        scratch_shapes=[pltpu.VMEM((s, SWA_KV), BF16), pltpu.VMEM((s, SWA_KV), BF16)],
        compiler_params=_params(("parallel",)),
        name="swa",
    )(sinks, proj, proj, proj)


SB_TQ = 256
SB_KPI = 2
LOG2E = math.log2(math.e)
SB_DEAD = -256.0


SB_PAIRS = 4


def _sb_kernel(q_ref, k_ref, v_ref, o_ref, kk_scr, vv_scr, carry_scr, acc_scr):
    s = q_ref.shape[1]
    nt = s // SB_TQ
    nb = s // BLK
    kpt = SB_TQ // BLK
    lane = lax.broadcasted_iota(jnp.int32, (BLK, BLK), 1)
    key_pos = lax.broadcasted_iota(jnp.int32, (SB_TQ, 2 * BLK), 1) & (BLK - 1)
    q_pos = lax.broadcasted_iota(jnp.int32, (SB_TQ, 2 * BLK), 0)
    jj = lax.broadcasted_iota(jnp.int32, (2 * BLK, 2 * BLK), 0) & (BLK - 1)
    ss = lax.broadcasted_iota(jnp.int32, (2 * BLK, 2 * BLK), 1)
    suffix_mat = jnp.where((jj > ss) | (ss >= BLK), 1.0, 0.0).astype(BF16)
    scale = HEAD_DIM ** -0.5

    def prep(j, _):
        rows = pl.ds(pl.multiple_of(j * BLK, BLK), BLK)
        for p in range(SB_PAIRS):
            kb = k_ref[0, rows, p * BLK:(p + 1) * BLK]
            vb = v_ref[0, rows, p * BLK:(p + 1) * BLK]
            kk_scr[p, j] = jnp.concatenate([jnp.where(lane < HEAD_DIM, kb, 0.0),
                                            jnp.where(lane >= HEAD_DIM, kb, 0.0)], axis=0).astype(BF16)
            vv_scr[p, j] = jnp.concatenate([jnp.where(lane < HEAD_DIM, vb, 0.0),
                                            jnp.where(lane >= HEAD_DIM, vb, 0.0)], axis=0).astype(BF16)
        return 0

    lax.fori_loop(0, nb, prep, 0)

    def key_blocks(qn, t, js, masked):
        first_row = [(len(js) - 1 - d) * BLK if masked else 0 for d in range(len(js))]
        chains = [(p, d) for p in range(SB_PAIRS) for d in range(len(js))]
        causal = [((js[d] * BLK + key_pos) < (t * SB_TQ + q_pos))[first_row[d]:] if masked else None
                  for d in range(len(js))]
        nz = {(p, d): _dot_nt(qn[p][first_row[d]:], kk_scr[p, js[d]]) for p, d in chains}
        log_beta, lhs = {}, {}
        for c in chains:
            log_keep = jnp.minimum(nz[c], 0.0) - jnp.log2(1.0 + jnp.exp2(-jnp.abs(nz[c])))
            log_beta[c] = log_keep - nz[c]
            if masked:
                log_keep = jnp.where(causal[c[1]], log_keep, 0.0)
            hi = log_keep.astype(BF16)
            lo = (log_keep - hi.astype(F32)).astype(BF16)
            lhs[c] = [jnp.concatenate([hi[:, h * BLK:(h + 1) * BLK], lo[:, h * BLK:(h + 1) * BLK]], axis=1)
                      for h in range(2)]
        cs = {c: [_dot(lhs[c][h], suffix_mat) for h in range(2)] for c in chains}
        w = {}
        for p in range(SB_PAIRS):
            carry = carry_scr[p]
            for d in range(len(js)):
                c, r0 = (p, d), first_row[d]
                between = jnp.concatenate([cs[c][0][:, :BLK], cs[c][1][:, :BLK]], axis=1)
                wc = jnp.exp2(log_beta[c] + between + carry[r0:])
                if masked:
                    wc = jnp.where(causal[d], wc, 0.0)
                w[c] = wc.astype(BF16)
                below = carry[r0:] + jnp.concatenate([cs[c][0][:, BLK:], cs[c][1][:, BLK:]], axis=1)
                carry = below if r0 == 0 else jnp.concatenate([carry[:r0], below], axis=0)
            carry_scr[p] = carry
        for p in range(SB_PAIRS):
            acc = acc_scr[p]
            for d, j in enumerate(js):
                r0 = first_row[d]
                below = acc[r0:] + _dot(w[(p, d)], vv_scr[p, j])
                acc = below if r0 == 0 else jnp.concatenate([acc[:r0], below], axis=0)
            acc_scr[p] = acc

    def q_tile(t, _):
        rows = pl.ds(pl.multiple_of(t * SB_TQ, SB_TQ), SB_TQ)
        qn = [(q_ref[0, rows, p * BLK:(p + 1) * BLK].astype(F32) * (-scale * LOG2E)).astype(BF16)
              for p in range(SB_PAIRS)]
        carry_scr[...] = jnp.zeros_like(carry_scr)
        acc_scr[...] = jnp.zeros_like(acc_scr)
        key_blocks(qn, t, [kpt * t + kpt - 1 - d for d in range(kpt)], True)

        def live(state):
            return (state[0] < kpt * t // SB_KPI) & (state[1] > SB_DEAD)

        def kv_step(state):
            i = state[0]
            key_blocks(qn, t, [kpt * t - 1 - i * SB_KPI - d for d in range(SB_KPI)], False)
            return i + 1, jnp.max(carry_scr[...])

        lax.while_loop(live, kv_step, (jnp.int32(0), jnp.float32(0.0)))
        for p in range(SB_PAIRS):
            o_ref[0, rows, p * BLK:(p + 1) * BLK] = acc_scr[p].astype(o_ref.dtype)
        return 0

    lax.fori_loop(0, nt, q_tile, 0)


def _sb(proj):
    b, s, _ = proj.shape
    assert (SB_TQ // BLK) % SB_KPI == 0
    width = SB_PAIRS * BLK
    q_col = 0
    k_col = SB_W // width
    v_col = 2 * SB_W // width
    return pl.pallas_call(
        _sb_kernel,
        grid=(b, SB_W // width),
        in_specs=[
            pl.BlockSpec((1, s, width), lambda i, p: (i, 0, q_col + p)),
            pl.BlockSpec((1, s, width), lambda i, p: (i, 0, k_col + p)),
            pl.BlockSpec((1, s, width), lambda i, p: (i, 0, v_col + p)),
        ],
        out_specs=pl.BlockSpec((1, s, width), lambda i, p: (i, 0, p)),
        out_shape=jax.ShapeDtypeStruct((b, s, SB_W), BF16),
        scratch_shapes=[pltpu.VMEM((SB_PAIRS, s // BLK, 2 * BLK, BLK), BF16),
                        pltpu.VMEM((SB_PAIRS, s // BLK, 2 * BLK, BLK), BF16),
                        pltpu.VMEM((SB_PAIRS, SB_TQ, 2 * BLK), F32), pltpu.VMEM((SB_PAIRS, SB_TQ, BLK), F32)],
        compiler_params=_params(("parallel", "parallel")),
        name="stick_breaking",
    )(proj, proj, proj)


def _hgrn_kernel(layer, lbl_ref, ng_ref, q_ref, f_ref, i_ref, gate_ref, o_ref, state_scr):
    nc = q_ref.shape[1] // HG_CHUNK
    group_chunks = math.gcd(nc, HG_GROUP)
    logits = lbl_ref[...]
    e = jnp.exp(logits - jnp.max(logits, axis=0, keepdims=True))
    sm = e / jnp.sum(e, axis=0, keepdims=True)
    lb = jnp.sum(sm[1:layer + 1], axis=0, keepdims=True)
    ti = lax.broadcasted_iota(jnp.int32, (HG_CHUNK, HG_CHUNK), 0)
    si = lax.broadcasted_iota(jnp.int32, (HG_CHUNK, HG_CHUNK), 1)
    cumsum_mat = jnp.where(si <= ti, 1.0, 0.0).astype(BF16)
    row8 = lax.broadcasted_iota(jnp.int32, (SUBLANES, 1), 0)
    n_sub = HG_CHUNK // HG_SUB
    state_scr[...] = jnp.zeros_like(state_scr)

    def pair_rows(acc, qt, gt, ks, gs, vs, s):
        decay = jnp.exp2(gt - gs[s:s + 1, :])
        col = jnp.sum(qt * (ks[s:s + 1, :] * decay), axis=-1, keepdims=True)
        col = jnp.where(row8 >= s, col, 0.0)
        return acc + col * vs[s:s + 1, :]

    def chunk(c, _):
        rows = pl.ds(pl.multiple_of(c * HG_CHUNK, HG_CHUNK), HG_CHUNK)
        z = f_ref[0, rows, :]
        qx = q_ref[0, rows, :].astype(F32)
        v_all = i_ref[0, rows, :].astype(F32)
        gate = gate_ref[0, rows, :].astype(F32)
        sig = _sigmoid(z)
        q_all = qx * _sigmoid(qx)
        k_all = (1.0 - lb) * (1.0 - sig)
        log2_f = jnp.log2(lb + (1.0 - lb) * sig)
        g_all = _split_dot_left(cumsum_mat, log2_f)
        g_last_all = g_all[HG_CHUNK - 1:HG_CHUNK, :]
        gated = gate * _sigmoid(gate)
        heads = range(HG_HEADS)
        cols = [slice(h * HG_DK, (h + 1) * HG_DK) for h in heads]
        v_bf = v_all.astype(BF16)
        state = [state_scr[h] for h in heads]
        qg_all = (q_all * jnp.exp2(g_all)).astype(BF16)
        kd_all = (k_all * jnp.exp2(g_last_all - g_all)).astype(BF16)
        o_inter = [_dot_nt(qg_all[:, cols[h]], state[h].astype(BF16)) for h in heads]
        new_state = [state[h] * jnp.exp2(g_last_all[:, cols[h]]) + _dot_tn(v_bf[:, cols[h]], kd_all[:, cols[h]])
                     for h in heads]
        kt_all, qt_all = [], []
        for j in range(n_sub - 1):
            lo, hi = j * HG_SUB, (j + 1) * HG_SUB
            r = g_all[hi - 1:hi, :]
            kt_all.append((k_all[lo:hi] * jnp.exp2(r - g_all[lo:hi])).astype(BF16))
            qt_all.append((q_all[hi:] * jnp.exp2(g_all[hi:] - r)).astype(BF16))
        q_low, k_up, v_up = [], [], []
        for i in range(n_sub):
            lo, mid, hi = i * HG_SUB, i * HG_SUB + SUBLANES, (i + 1) * HG_SUB
            r = g_all[mid - 1:mid, :]
            q_low.append(q_all[mid:hi] * jnp.exp2(g_all[mid:hi] - r))
            k_up.append(k_all[lo:mid] * jnp.exp2(r - g_all[lo:mid]))
            v_up.append(v_all[lo:mid])
        q_low = jnp.concatenate(q_low, axis=0).astype(BF16)
        k_up = jnp.concatenate(k_up, axis=0).astype(BF16)
        v_up = jnp.concatenate(v_up, axis=0).astype(BF16)
        same_sub = (lax.broadcasted_iota(jnp.int32, (n_sub * SUBLANES, n_sub * SUBLANES), 0) // SUBLANES
                    == lax.broadcasted_iota(jnp.int32, (n_sub * SUBLANES, n_sub * SUBLANES), 1) // SUBLANES)
        sc = [[_dot_nt(qt_all[j][:, cols[h]], kt_all[j][:, cols[h]]).astype(BF16)
               for j in range(n_sub - 1)] for h in heads]
        contrib = [[_dot(sc[h][j], v_bf[j * HG_SUB:(j + 1) * HG_SUB, cols[h]])
                    for j in range(n_sub - 1)] for h in heads]
        sc_low = [jnp.where(same_sub, _dot_nt(q_low[:, cols[h]], k_up[:, cols[h]]), 0.0).astype(BF16) for h in heads]
        contrib_low = [_dot(sc_low[h], v_up[:, cols[h]]) for h in heads]
        tiles = [[None] * (HG_CHUNK // SUBLANES) for _ in heads]
        for h in heads:
            for i in range(n_sub):
                part = jnp.zeros((HG_SUB, HG_DK), F32)
                for j in range(i):
                    part = part + contrib[h][j][(i - j - 1) * HG_SUB:(i - j) * HG_SUB]
                tiles[h][2 * i] = part[:SUBLANES]
                tiles[h][2 * i + 1] = part[SUBLANES:] + contrib_low[h][i * SUBLANES:(i + 1) * SUBLANES]
        for i in range(2 * n_sub):
            a = slice(i * SUBLANES, (i + 1) * SUBLANES)
            for s in range(SUBLANES):
                for h in heads:
                    g, q, k, v = g_all[:, cols[h]], q_all[:, cols[h]], k_all[:, cols[h]], v_all[:, cols[h]]
                    tiles[h][i] = pair_rows(tiles[h][i], q[a], g[a], k[a], g[a], v[a], s)
        for h in heads:
            o = o_inter[h] + jnp.concatenate(tiles[h], axis=0)
            state_scr[h] = new_state[h]
            o_ref[0, rows, cols[h]] = (_rms(o, ng_ref[...]) * gated[:, cols[h]]).astype(o_ref.dtype)
        return 0

    def chunk_group(gi, _):
        n_rows = group_chunks * HG_CHUNK
        rows = pl.ds(pl.multiple_of(gi * n_rows, n_rows), n_rows)
        part = [slice(cc * HG_CHUNK, (cc + 1) * HG_CHUNK) for cc in range(group_chunks)]
        heads = range(HG_HEADS)
        cols = [slice(h * HG_DK, (h + 1) * HG_DK) for h in heads]
        pairs = [(cc, h) for cc in range(group_chunks) for h in heads]
        z = f_ref[0, rows, :]
        qx = q_ref[0, rows, :].astype(F32)
        v_bf = i_ref[0, rows, :]
        gate = gate_ref[0, rows, :].astype(F32)
        sig = _sigmoid(z)
        q_all = qx * _sigmoid(qx)
        k_all = (1.0 - lb) * (1.0 - sig)
        log2_f = jnp.log2(lb + (1.0 - lb) * sig)
        gated = gate * _sigmoid(gate)
        hi = log2_f.astype(BF16)
        lo = (log2_f - hi.astype(F32)).astype(BF16)
        g = [_dot(cumsum_mat, hi[p]) + _dot(cumsum_mat, lo[p]) for p in part]
        r = [0.5 * gc[HG_CHUNK - 1:HG_CHUNK, :] for gc in g]
        half = [jnp.exp2(rc) for rc in r]
        centred = jnp.concatenate([gc - rc for gc, rc in zip(g, r)], axis=0)
        qs = q_all * jnp.exp2(centred)
        ks = k_all * jnp.exp2(-centred)
        qs_bf, ks_bf = qs.astype(BF16), ks.astype(BF16)
        qg = [(qs[p] * hc).astype(BF16) for p, hc in zip(part, half)]
        kd = [(ks[p] * hc).astype(BF16) for p, hc in zip(part, half)]
        update = {(cc, h): _dot_tn(v_bf[part[cc], cols[h]], kd[cc][:, cols[h]]) for cc, h in pairs}
        sc = {(cc, h): jnp.where(si <= ti, _dot_nt(qs_bf[part[cc], cols[h]], ks_bf[part[cc], cols[h]]), 0.0
                                 ).astype(BF16) for cc, h in pairs}
        intra = {(cc, h): _dot(sc[cc, h], v_bf[part[cc], cols[h]]) for cc, h in pairs}
        state = [state_scr[h] for h in heads]
        inter = {}
        for cc in range(group_chunks):
            for h in heads:
                inter[cc, h] = _dot_nt(qg[cc][:, cols[h]], state[h].astype(BF16))
                state[h] = state[h] * (half[cc][:, cols[h]] * half[cc][:, cols[h]]) + update[cc, h]
        for h in heads:
            state_scr[h] = state[h]
        for cc, h in pairs:
            out_rows = pl.ds(pl.multiple_of(gi * n_rows + cc * HG_CHUNK, HG_CHUNK), HG_CHUNK)
            o_ref[0, out_rows, cols[h]] = (_rms(inter[cc, h] + intra[cc, h], ng_ref[...]) * gated[part[cc], cols[h]]
                                           ).astype(o_ref.dtype)
        return 0

    def chunk_decay(c, worst):
        rows = pl.ds(pl.multiple_of(c * HG_CHUNK, HG_CHUNK), HG_CHUNK)
        log2_f = jnp.log2(lb + (1.0 - lb) * _sigmoid(f_ref[0, rows, :]))
        return jnp.minimum(worst, jnp.sum(log2_f, axis=0, keepdims=True))

    worst = jnp.min(lax.fori_loop(0, nc, chunk_decay, jnp.zeros((1, HG_WIDTH), F32), unroll=HG_UNROLL))
    gentle = worst >= -HG_SAFE_DECAY

    @pl.when(gentle)
    def _():
        lax.fori_loop(0, nc // group_chunks, chunk_group, 0)

    @pl.when(jnp.logical_not(gentle))
    def _():
        lax.fori_loop(0, nc, chunk, 0, unroll=HG_UNROLL)


def _hgrn(proj, f_logit, lb_logits, norm_g, layer):
    b, s, _ = proj.shape
    depth = lb_logits.shape[0]
    assert HG_SUB == 2 * SUBLANES
    blk = lambda off: pl.BlockSpec((1, s, HG_WIDTH), lambda i: (i, 0, off))
    return pl.pallas_call(
        functools.partial(_hgrn_kernel, layer),
        grid=(b,),
        in_specs=[
            pl.BlockSpec((depth, HG_WIDTH), lambda i: (0, 0)),
            pl.BlockSpec((1, HG_DK), lambda i: (0, 0)),
            blk(0), blk(0), blk(2), blk(3),
        ],
        out_specs=pl.BlockSpec((1, s, HG_WIDTH), lambda i: (i, 0, 0)),
        out_shape=jax.ShapeDtypeStruct((b, s, HG_WIDTH), BF16),
        scratch_shapes=[pltpu.VMEM((HG_HEADS, HG_DK, HG_DK), F32)],
        compiler_params=_params(("parallel",)),
        name="hgrn2",
    )(lb_logits, norm_g.reshape(1, HG_DK), proj, f_logit, proj, proj)


def _attn_columns(w_in):
    split = SWA_Q + 2 * SWA_KV
    return jnp.concatenate([w_in[:, split:], w_in[:, :split]], axis=1)


def kernel(x, norm_g, ffn_w_in, ffn_w_out, attn_w_in, attn_sinks, attn_w_out, rec_w_in,
           hgrn_lb_logits, hgrn_norm_g, conv_w, rec_w_out, final_g):
    b, s, d = x.shape
    depth = norm_g.shape[0]
    m = b * s
    h = x.reshape(m, d)
    w_in_bf, w_out_bf = ffn_w_in.astype(BF16), ffn_w_out.astype(BF16)
    for layer in range(depth):
        h = _ffn(h, norm_g[layer, 0], w_in_bf, w_out_bf, layer, 0, final_g, final_norm=False)
        if layer % 2 == 0:
            e = layer // 2
            proj = _proj_in(h, norm_g[layer, 1], _attn_columns(attn_w_in[e]).astype(BF16)).reshape(b, s, ATTN_IN)
            o_a = _swa(proj, attn_sinks[e])
            o_b = _sb(proj)
            mix = (o_a.reshape(m, SWA_Q), o_b.reshape(m, SB_W), attn_w_out[e].astype(BF16))
        else:
            o = layer // 2
            proj, f_logit, o_d = _proj_rec(h, norm_g[layer, 1], rec_w_in[o].astype(BF16), conv_w[o], s)
            o_c = _hgrn(proj.reshape(b, s, 4 * HG_WIDTH), f_logit.reshape(b, s, HG_WIDTH), hgrn_lb_logits,
                        hgrn_norm_g[o], layer)
            mix = (o_c.reshape(m, HG_WIDTH), o_d, rec_w_out[o].astype(BF16))
        h = _ffn(h, norm_g[layer, 2], w_in_bf, w_out_bf, layer, 1, final_g,
                 final_norm=(layer == depth - 1), mix=mix)
    return h.reshape(b, s, d)
```

```python
import functools
import math

import jax
import jax.numpy as jnp
from jax import lax
from jax.experimental import pallas as pl
from jax.experimental.pallas import tpu as pltpu

F32 = jnp.float32
BF16 = jnp.bfloat16

HEAD_DIM = 64
SWA_HEADS = 8
SWA_KV_HEADS = 2
SWA_GROUP = SWA_HEADS // SWA_KV_HEADS
WINDOW = 128
BLK = 128
SUBLANES = 8
SB_HEADS = 8
HG_WIDTH = 512
HG_DK = 128
HG_HEADS = HG_WIDTH // HG_DK
HG_CHUNK = 64
HG_SUB = 16
HG_UNROLL = 4
HG_GROUP = 8
HG_SAFE_DECAY = 200.0
SWA_UNROLL = 4
CONV_WIDTH = 512
CONV_K = 3
EPS = 1e-6

SWA_Q = SWA_HEADS * HEAD_DIM
SWA_KV = SWA_KV_HEADS * HEAD_DIM
SB_W = SB_HEADS * HEAD_DIM
ATTN_IN = SWA_Q + 2 * SWA_KV + 3 * SB_W
REC_IN = 4 * HG_WIDTH + 3 * CONV_WIDTH

VMEM_LIMIT_BYTES = 56 * 1024 * 1024


def _params(semantics):
    return pltpu.CompilerParams(dimension_semantics=semantics,
                                vmem_limit_bytes=VMEM_LIMIT_BYTES)


def _rms(x, g):
    return x * lax.rsqrt(jnp.mean(x * x, axis=-1, keepdims=True) + EPS) * g


def _sigmoid(x):
    return 0.5 * (1.0 + jnp.tanh(0.5 * x))


def _dot(a, b):
    return jnp.dot(a, b, preferred_element_type=F32)


def _dot_nt(a, b):
    return lax.dot_general(a, b, (((1,), (1,)), ((), ())), preferred_element_type=F32)


def _dot_tn(a, b):
    return lax.dot_general(a, b, (((0,), (0,)), ((), ())), preferred_element_type=F32)


def _split_dot_left(m01, x):
    hi = x.astype(BF16)
    lo = (x - hi.astype(F32)).astype(BF16)
    return _dot(m01, hi) + _dot(m01, lo)


def _ffn_kernel(*refs, final_norm, with_mix):
    if with_mix:
        x_ref, a_ref, b_ref, wm_ref, g_ref, wi_ref, wo_ref, fg_ref, o_ref = refs
    else:
        x_ref, g_ref, wi_ref, wo_ref, fg_ref, o_ref = refs
    d_ff = wo_ref.shape[0]
    x = x_ref[...]
    if with_mix:
        ka = a_ref.shape[1]
        x = x + _dot(a_ref[...], wm_ref[:ka, :]) + _dot(b_ref[...], wm_ref[ka:, :])
    n = _rms(x, g_ref[...]).astype(BF16)
    gate = _dot(n, wi_ref[:, :d_ff])
    up = _dot(n, wi_ref[:, d_ff:])
    act = (gate * jax.nn.sigmoid(gate) * up).astype(BF16)
    h = x + 0.5 * _dot(act, wo_ref[...])
    if final_norm:
        h = _rms(h, fg_ref[...])
    o_ref[...] = h


FFN_TM = 512


def _ffn(h, g, w_in_all, w_out_all, layer, which, final_g, *, final_norm, mix=None):
    m, d = h.shape
    d_ff = w_out_all.shape[2]
    tm = min(FFN_TM, m)
    rows = lambda width: pl.BlockSpec((tm, width), lambda i: (i, 0))
    resident = lambda shape: pl.BlockSpec(shape, lambda i: (0, 0), pipeline_mode=pl.Buffered(1))
    stacked = lambda shape: pl.BlockSpec((None, None) + shape, lambda i: (layer, which, 0, 0),
                                         pipeline_mode=pl.Buffered(1))
    operands, in_specs = [h], [rows(d)]
    if mix is not None:
        a, b, w_mix = mix
        operands += [a, b, w_mix]
        in_specs += [rows(a.shape[1]), rows(b.shape[1]), resident(w_mix.shape)]
    operands += [g.reshape(1, d), w_in_all, w_out_all, final_g.reshape(1, d)]
    in_specs += [resident((1, d)), stacked((d, 2 * d_ff)), stacked((d_ff, d)), resident((1, d))]
    return pl.pallas_call(
        functools.partial(_ffn_kernel, final_norm=final_norm, with_mix=mix is not None),
        grid=(m // tm,),
        in_specs=in_specs,
        out_specs=rows(d),
        out_shape=jax.ShapeDtypeStruct((m, d), F32),
        compiler_params=_params(("parallel",)),
        name="ffn",
    )(*operands)


def _proj_in_kernel(x_ref, g_ref, w_ref, o_ref):
    o_ref[...] = _dot(_rms(x_ref[...], g_ref[...]).astype(BF16), w_ref[...]).astype(o_ref.dtype)


PROJ_TM = 512


def _proj_in(h, g, w):
    m, d = h.shape
    n_out = w.shape[1]
    tm = min(PROJ_TM, m)
    resident = lambda shape: pl.BlockSpec(shape, lambda i: (0, 0), pipeline_mode=pl.Buffered(1))
    return pl.pallas_call(
        _proj_in_kernel,
        grid=(m // tm,),
        in_specs=[pl.BlockSpec((tm, d), lambda i: (i, 0)), resident((1, d)), resident((d, n_out))],
        out_specs=pl.BlockSpec((tm, n_out), lambda i: (i, 0)),
        out_shape=jax.ShapeDtypeStruct((m, n_out), BF16),
        compiler_params=_params(("parallel",)),
        name="proj_in",
    )(h, g.reshape(1, d), w)


def _proj_rec_kernel(x_ref, g_ref, w_ref, cw_ref, o_ref, f_ref, od_ref, tail_scr, *, seq_tiles):
    y = _dot(_rms(x_ref[...], g_ref[...]).astype(BF16), w_ref[...])
    hg = 4 * HG_WIDTH
    o_ref[...] = y[:, :hg].astype(o_ref.dtype)
    f_ref[...] = y[:, HG_WIDTH:2 * HG_WIDTH]
    gate_b = y[:, hg:hg + CONV_WIDTH]
    x = y[:, hg + CONV_WIDTH:hg + 2 * CONV_WIDTH] * y[:, hg + 2 * CONV_WIDTH:]
    first = pl.program_id(0) % seq_tiles == 0
    tail = jnp.where(first, 0.0, tail_scr[...])
    row = lax.broadcasted_iota(jnp.int32, x.shape, 0)
    t1 = tail[SUBLANES - 1:SUBLANES, :]
    t2 = tail[SUBLANES - 2:SUBLANES - 1, :]
    x1 = jnp.where(row == 0, t1, pltpu.roll(x, 1, 0))
    x2 = jnp.where(row == 0, t2, jnp.where(row == 1, t1, pltpu.roll(x, 2, 0)))
    conv = cw_ref[0:1, :] * x2 + cw_ref[1:2, :] * x1 + cw_ref[2:3, :] * x
    od_ref[...] = (gate_b * conv).astype(od_ref.dtype)
    tail_scr[...] = x[x.shape[0] - SUBLANES:, :]


def _proj_rec(h, g, w, conv_w, seq_len):
    m, d = h.shape
    tm = min(PROJ_TM, seq_len)
    assert CONV_K - 1 <= SUBLANES and seq_len % tm == 0
    resident = lambda shape: pl.BlockSpec(shape, lambda i: (0, 0), pipeline_mode=pl.Buffered(1))
    rows = lambda width: pl.BlockSpec((tm, width), lambda i: (i, 0))
    return pl.pallas_call(
        functools.partial(_proj_rec_kernel, seq_tiles=seq_len // tm),
        grid=(m // tm,),
        in_specs=[rows(d), resident((1, d)), resident((d, REC_IN)), resident((CONV_K, CONV_WIDTH))],
        out_specs=[rows(4 * HG_WIDTH), rows(HG_WIDTH), rows(CONV_WIDTH)],
        out_shape=[jax.ShapeDtypeStruct((m, 4 * HG_WIDTH), BF16), jax.ShapeDtypeStruct((m, HG_WIDTH), F32),
                   jax.ShapeDtypeStruct((m, CONV_WIDTH), BF16)],
        scratch_shapes=[pltpu.VMEM((SUBLANES, CONV_WIDTH), F32)],
        compiler_params=_params(("arbitrary",)),
        name="proj_rec",
    )(h, g.reshape(1, d), w, conv_w)


def _swa_kernel(sink_ref, q_ref, k_ref, v_ref, o_ref, k_swapped, v_swapped):
    nb = q_ref.shape[1] // BLK
    half = HEAD_DIM
    k_swapped[...] = pltpu.roll(k_ref[0].astype(F32), half, 1).astype(BF16)
    v_swapped[...] = pltpu.roll(v_ref[0].astype(F32), half, 1).astype(BF16)
    qi = lax.broadcasted_iota(jnp.int32, (BLK, 2 * BLK), 0)
    kj = lax.broadcasted_iota(jnp.int32, (BLK, 2 * BLK), 1)
    dist = qi + BLK - kj
    in_window = (dist >= 0) & (dist < WINDOW)
    dist_f = dist.astype(F32)
    lane = lax.broadcasted_iota(jnp.int32, (BLK, BLK), 1)
    scale = HEAD_DIM ** -0.5

    def q_block(n, _):
        cur = pl.ds(pl.multiple_of(n * BLK, BLK), BLK)
        prev = pl.ds(pl.multiple_of(jnp.maximum(n - 1, 0) * BLK, BLK), BLK)
        valid = in_window & (n * BLK - BLK + kj >= 0)
        penalty = jnp.where(valid, dist_f, jnp.inf)
        band = lambda ref: jnp.concatenate([ref[prev, :], ref[cur, :]], axis=0)
        k_by_half = (band(k_ref.at[0]), band(k_swapped))
        v_by_half = (band(v_ref.at[0]), band(v_swapped))
        heads = range(SWA_HEADS)
        kv_half = [(h // SWA_GROUP) ^ (h % 2) for h in heads]
        scores = []
        for h in heads:
            q_col = q_ref[0, cur, (h // 2) * BLK:(h // 2 + 1) * BLK].astype(F32) * scale
            in_half = (lane >= (h % 2) * half) & (lane < (h % 2 + 1) * half)
            qm = jnp.where(in_half, q_col, 0.0).astype(BF16)
            slope = 2.0 ** (-8.0 * (h + 1) / SWA_HEADS)
            scores.append(_dot_nt(qm, k_by_half[kv_half[h]]) - slope * penalty)
        row_max = [jnp.maximum(jnp.max(scores[h], axis=-1, keepdims=True), sink_ref[h]) for h in heads]
        probs = [jnp.exp(scores[h] - row_max[h]) for h in heads]
        num = [_dot(probs[h].astype(BF16), v_by_half[kv_half[h]]) for h in heads]
        den = [jnp.sum(probs[h], axis=-1, keepdims=True) + jnp.exp(sink_ref[h] - row_max[h]) for h in heads]
        out = [num[h] / den[h] for h in heads]
        for col in range(SWA_Q // BLK):
            o_ref[0, cur, col * BLK:(col + 1) * BLK] = jnp.where(
                lane < half, out[2 * col], out[2 * col + 1]).astype(o_ref.dtype)
        return 0

    lax.fori_loop(0, nb, q_block, 0, unroll=SWA_UNROLL)


def _swa(proj, sinks):
    b, s, _ = proj.shape
    assert SWA_KV == BLK and SWA_KV_HEADS == 2
    q_col = 3 * SB_W // SWA_Q
    k_col = (3 * SB_W + SWA_Q) // SWA_KV
    v_col = k_col + 1
    return pl.pallas_call(
        _swa_kernel,
        grid=(b,),
        in_specs=[
            pl.BlockSpec(memory_space=pltpu.SMEM),
            pl.BlockSpec((1, s, SWA_Q), lambda i: (i, 0, q_col)),
            pl.BlockSpec((1, s, SWA_KV), lambda i: (i, 0, k_col)),
            pl.BlockSpec((1, s, SWA_KV), lambda i: (i, 0, v_col)),
        ],
        out_specs=pl.BlockSpec((1, s, SWA_Q), lambda i: (i, 0, 0)),
        out_shape=jax.ShapeDtypeStruct((b, s, SWA_Q), BF16),
        scratch_shapes=[pltpu.VMEM((s, SWA_KV), BF16), pltpu.VMEM((s, SWA_KV), BF16)],
        compiler_params=_params(("parallel",)),
        name="swa",
    )(sinks, proj, proj, proj)


SB_TQ = 256
SB_KPI = 2
LOG2E = math.log2(math.e)
SB_DEAD = -256.0


SB_PAIRS = 4


def _sb_kernel(q_ref, k_ref, v_ref, o_ref, kk_scr, vv_scr, carry_scr, acc_scr):
    s = q_ref.shape[1]
    nt = s // SB_TQ
    nb = s // BLK
    kpt = SB_TQ // BLK
    lane = lax.broadcasted_iota(jnp.int32, (BLK, BLK), 1)
    key_pos = lax.broadcasted_iota(jnp.int32, (SB_TQ, 2 * BLK), 1) & (BLK - 1)
    q_pos = lax.broadcasted_iota(jnp.int32, (SB_TQ, 2 * BLK), 0)
    jj = lax.broadcasted_iota(jnp.int32, (2 * BLK, 2 * BLK), 0) & (BLK - 1)
    ss = lax.broadcasted_iota(jnp.int32, (2 * BLK, 2 * BLK), 1)
    suffix_mat = jnp.where((jj > ss) | (ss >= BLK), 1.0, 0.0).astype(BF16)
    scale = HEAD_DIM ** -0.5

    def prep(j, _):
        rows = pl.ds(pl.multiple_of(j * BLK, BLK), BLK)
        for p in range(SB_PAIRS):
            kb = k_ref[0, rows, p * BLK:(p + 1) * BLK]
            vb = v_ref[0, rows, p * BLK:(p + 1) * BLK]
            kk_scr[p, j] = jnp.concatenate([jnp.where(lane < HEAD_DIM, kb, 0.0),
                                            jnp.where(lane >= HEAD_DIM, kb, 0.0)], axis=0).astype(BF16)
            vv_scr[p, j] = jnp.concatenate([jnp.where(lane < HEAD_DIM, vb, 0.0),
                                            jnp.where(lane >= HEAD_DIM, vb, 0.0)], axis=0).astype(BF16)
        return 0

    lax.fori_loop(0, nb, prep, 0)

    def key_blocks(qn, t, js, masked):
        first_row = [(len(js) - 1 - d) * BLK if masked else 0 for d in range(len(js))]
        chains = [(p, d) for p in range(SB_PAIRS) for d in range(len(js))]
        causal = [((js[d] * BLK + key_pos) < (t * SB_TQ + q_pos))[first_row[d]:] if masked else None
                  for d in range(len(js))]
        nz = {(p, d): _dot_nt(qn[p][first_row[d]:], kk_scr[p, js[d]]) for p, d in chains}
        log_beta, lhs = {}, {}
        for c in chains:
            log_keep = jnp.minimum(nz[c], 0.0) - jnp.log2(1.0 + jnp.exp2(-jnp.abs(nz[c])))
            log_beta[c] = log_keep - nz[c]
            if masked:
                log_keep = jnp.where(causal[c[1]], log_keep, 0.0)
            hi = log_keep.astype(BF16)
            lo = (log_keep - hi.astype(F32)).astype(BF16)
            lhs[c] = [jnp.concatenate([hi[:, h * BLK:(h + 1) * BLK], lo[:, h * BLK:(h + 1) * BLK]], axis=1)
                      for h in range(2)]
        cs = {c: [_dot(lhs[c][h], suffix_mat) for h in range(2)] for c in chains}
        w = {}
        for p in range(SB_PAIRS):
            carry = carry_scr[p]
            for d in range(len(js)):
                c, r0 = (p, d), first_row[d]
                between = jnp.concatenate([cs[c][0][:, :BLK], cs[c][1][:, :BLK]], axis=1)
                wc = jnp.exp2(log_beta[c] + between + carry[r0:])
                if masked:
                    wc = jnp.where(causal[d], wc, 0.0)
                w[c] = wc.astype(BF16)
                below = carry[r0:] + jnp.concatenate([cs[c][0][:, BLK:], cs[c][1][:, BLK:]], axis=1)
                carry = below if r0 == 0 else jnp.concatenate([carry[:r0], below], axis=0)
            carry_scr[p] = carry
        for p in range(SB_PAIRS):
            acc = acc_scr[p]
            for d, j in enumerate(js):
                r0 = first_row[d]
                below = acc[r0:] + _dot(w[(p, d)], vv_scr[p, j])
                acc = below if r0 == 0 else jnp.concatenate([acc[:r0], below], axis=0)
            acc_scr[p] = acc

    def q_tile(t, _):
        rows = pl.ds(pl.multiple_of(t * SB_TQ, SB_TQ), SB_TQ)
        qn = [(q_ref[0, rows, p * BLK:(p + 1) * BLK].astype(F32) * (-scale * LOG2E)).astype(BF16)
              for p in range(SB_PAIRS)]
        carry_scr[...] = jnp.zeros_like(carry_scr)
        acc_scr[...] = jnp.zeros_like(acc_scr)
        key_blocks(qn, t, [kpt * t + kpt - 1 - d for d in range(kpt)], True)

        def live(state):
            return (state[0] < kpt * t // SB_KPI) & (state[1] > SB_DEAD)

        def kv_step(state):
            i = state[0]
            key_blocks(qn, t, [kpt * t - 1 - i * SB_KPI - d for d in range(SB_KPI)], False)
            return i + 1, jnp.max(carry_scr[...])

        lax.while_loop(live, kv_step, (jnp.int32(0), jnp.float32(0.0)))
        for p in range(SB_PAIRS):
            o_ref[0, rows, p * BLK:(p + 1) * BLK] = acc_scr[p].astype(o_ref.dtype)
        return 0

    lax.fori_loop(0, nt, q_tile, 0)


def _sb(proj):
    b, s, _ = proj.shape
    assert (SB_TQ // BLK) % SB_KPI == 0
    width = SB_PAIRS * BLK
    q_col = 0
    k_col = SB_W // width
    v_col = 2 * SB_W // width
    return pl.pallas_call(
        _sb_kernel,
        grid=(b, SB_W // width),
        in_specs=[
            pl.BlockSpec((1, s, width), lambda i, p: (i, 0, q_col + p)),
            pl.BlockSpec((1, s, width), lambda i, p: (i, 0, k_col + p)),
            pl.BlockSpec((1, s, width), lambda i, p: (i, 0, v_col + p)),
        ],
        out_specs=pl.BlockSpec((1, s, width), lambda i, p: (i, 0, p)),
        out_shape=jax.ShapeDtypeStruct((b, s, SB_W), BF16),
        scratch_shapes=[pltpu.VMEM((SB_PAIRS, s // BLK, 2 * BLK, BLK), BF16),
                        pltpu.VMEM((SB_PAIRS, s // BLK, 2 * BLK, BLK), BF16),
                        pltpu.VMEM((SB_PAIRS, SB_TQ, 2 * BLK), F32), pltpu.VMEM((SB_PAIRS, SB_TQ, BLK), F32)],
        compiler_params=_params(("parallel", "parallel")),
        name="stick_breaking",
    )(proj, proj, proj)


def _hgrn_kernel(layer, lbl_ref, ng_ref, q_ref, f_ref, i_ref, gate_ref, o_ref, state_scr):
    nc = q_ref.shape[1] // HG_CHUNK
    group_chunks = math.gcd(nc, HG_GROUP)
    logits = lbl_ref[...]
    e = jnp.exp(logits - jnp.max(logits, axis=0, keepdims=True))
    sm = e / jnp.sum(e, axis=0, keepdims=True)
    lb = jnp.sum(sm[1:layer + 1], axis=0, keepdims=True)
    ti = lax.broadcasted_iota(jnp.int32, (HG_CHUNK, HG_CHUNK), 0)
    si = lax.broadcasted_iota(jnp.int32, (HG_CHUNK, HG_CHUNK), 1)
    cumsum_mat = jnp.where(si <= ti, 1.0, 0.0).astype(BF16)
    row8 = lax.broadcasted_iota(jnp.int32, (SUBLANES, 1), 0)
    n_sub = HG_CHUNK // HG_SUB
    state_scr[...] = jnp.zeros_like(state_scr)

    def pair_rows(acc, qt, gt, ks, gs, vs, s):
        decay = jnp.exp2(gt - gs[s:s + 1, :])
        col = jnp.sum(qt * (ks[s:s + 1, :] * decay), axis=-1, keepdims=True)
        col = jnp.where(row8 >= s, col, 0.0)
        return acc + col * vs[s:s + 1, :]

    def chunk(c, _):
        rows = pl.ds(pl.multiple_of(c * HG_CHUNK, HG_CHUNK), HG_CHUNK)
        z = f_ref[0, rows, :]
        qx = q_ref[0, rows, :].astype(F32)
        v_all = i_ref[0, rows, :].astype(F32)
        gate = gate_ref[0, rows, :].astype(F32)
        sig = _sigmoid(z)
        q_all = qx * _sigmoid(qx)
        k_all = (1.0 - lb) * (1.0 - sig)
        log2_f = jnp.log2(lb + (1.0 - lb) * sig)
        g_all = _split_dot_left(cumsum_mat, log2_f)
        g_last_all = g_all[HG_CHUNK - 1:HG_CHUNK, :]
        gated = gate * _sigmoid(gate)
        heads = range(HG_HEADS)
        cols = [slice(h * HG_DK, (h + 1) * HG_DK) for h in heads]
        v_bf = v_all.astype(BF16)
        state = [state_scr[h] for h in heads]
        qg_all = (q_all * jnp.exp2(g_all)).astype(BF16)
        kd_all = (k_all * jnp.exp2(g_last_all - g_all)).astype(BF16)
        o_inter = [_dot_nt(qg_all[:, cols[h]], state[h].astype(BF16)) for h in heads]
        new_state = [state[h] * jnp.exp2(g_last_all[:, cols[h]]) + _dot_tn(v_bf[:, cols[h]], kd_all[:, cols[h]])
                     for h in heads]
        kt_all, qt_all = [], []
        for j in range(n_sub - 1):
            lo, hi = j * HG_SUB, (j + 1) * HG_SUB
            r = g_all[hi - 1:hi, :]
            kt_all.append((k_all[lo:hi] * jnp.exp2(r - g_all[lo:hi])).astype(BF16))
            qt_all.append((q_all[hi:] * jnp.exp2(g_all[hi:] - r)).astype(BF16))
        q_low, k_up, v_up = [], [], []
        for i in range(n_sub):
            lo, mid, hi = i * HG_SUB, i * HG_SUB + SUBLANES, (i + 1) * HG_SUB
            r = g_all[mid - 1:mid, :]
            q_low.append(q_all[mid:hi] * jnp.exp2(g_all[mid:hi] - r))
            k_up.append(k_all[lo:mid] * jnp.exp2(r - g_all[lo:mid]))
            v_up.append(v_all[lo:mid])
        q_low = jnp.concatenate(q_low, axis=0).astype(BF16)
        k_up = jnp.concatenate(k_up, axis=0).astype(BF16)
        v_up = jnp.concatenate(v_up, axis=0).astype(BF16)
        same_sub = (lax.broadcasted_iota(jnp.int32, (n_sub * SUBLANES, n_sub * SUBLANES), 0) // SUBLANES
                    == lax.broadcasted_iota(jnp.int32, (n_sub * SUBLANES, n_sub * SUBLANES), 1) // SUBLANES)
        sc = [[_dot_nt(qt_all[j][:, cols[h]], kt_all[j][:, cols[h]]).astype(BF16)
               for j in range(n_sub - 1)] for h in heads]
        contrib = [[_dot(sc[h][j], v_bf[j * HG_SUB:(j + 1) * HG_SUB, cols[h]])
                    for j in range(n_sub - 1)] for h in heads]
        sc_low = [jnp.where(same_sub, _dot_nt(q_low[:, cols[h]], k_up[:, cols[h]]), 0.0).astype(BF16) for h in heads]
        contrib_low = [_dot(sc_low[h], v_up[:, cols[h]]) for h in heads]
        tiles = [[None] * (HG_CHUNK // SUBLANES) for _ in heads]
        for h in heads:
            for i in range(n_sub):
                part = jnp.zeros((HG_SUB, HG_DK), F32)
                for j in range(i):
                    part = part + contrib[h][j][(i - j - 1) * HG_SUB:(i - j) * HG_SUB]
                tiles[h][2 * i] = part[:SUBLANES]
                tiles[h][2 * i + 1] = part[SUBLANES:] + contrib_low[h][i * SUBLANES:(i + 1) * SUBLANES]
        for i in range(2 * n_sub):
            a = slice(i * SUBLANES, (i + 1) * SUBLANES)
            for s in range(SUBLANES):
                for h in heads:
                    g, q, k, v = g_all[:, cols[h]], q_all[:, cols[h]], k_all[:, cols[h]], v_all[:, cols[h]]
                    tiles[h][i] = pair_rows(tiles[h][i], q[a], g[a], k[a], g[a], v[a], s)
        for h in heads:
            o = o_inter[h] + jnp.concatenate(tiles[h], axis=0)
            state_scr[h] = new_state[h]
            o_ref[0, rows, cols[h]] = (_rms(o, ng_ref[...]) * gated[:, cols[h]]).astype(o_ref.dtype)
        return 0

    def chunk_group(gi, _):
        n_rows = group_chunks * HG_CHUNK
        rows = pl.ds(pl.multiple_of(gi * n_rows, n_rows), n_rows)
        part = [slice(cc * HG_CHUNK, (cc + 1) * HG_CHUNK) for cc in range(group_chunks)]
        heads = range(HG_HEADS)
        cols = [slice(h * HG_DK, (h + 1) * HG_DK) for h in heads]
        pairs = [(cc, h) for cc in range(group_chunks) for h in heads]
        z = f_ref[0, rows, :]
        qx = q_ref[0, rows, :].astype(F32)
        v_bf = i_ref[0, rows, :]
        gate = gate_ref[0, rows, :].astype(F32)
        sig = _sigmoid(z)
        q_all = qx * _sigmoid(qx)
        k_all = (1.0 - lb) * (1.0 - sig)
        log2_f = jnp.log2(lb + (1.0 - lb) * sig)
        gated = gate * _sigmoid(gate)
        hi = log2_f.astype(BF16)
        lo = (log2_f - hi.astype(F32)).astype(BF16)
        g = [_dot(cumsum_mat, hi[p]) + _dot(cumsum_mat, lo[p]) for p in part]
        r = [0.5 * gc[HG_CHUNK - 1:HG_CHUNK, :] for gc in g]
        half = [jnp.exp2(rc) for rc in r]
        centred = jnp.concatenate([gc - rc for gc, rc in zip(g, r)], axis=0)
        qs = q_all * jnp.exp2(centred)
        ks = k_all * jnp.exp2(-centred)
        qs_bf, ks_bf = qs.astype(BF16), ks.astype(BF16)
        qg = [(qs[p] * hc).astype(BF16) for p, hc in zip(part, half)]
        kd = [(ks[p] * hc).astype(BF16) for p, hc in zip(part, half)]
        update = {(cc, h): _dot_tn(v_bf[part[cc], cols[h]], kd[cc][:, cols[h]]) for cc, h in pairs}
        sc = {(cc, h): jnp.where(si <= ti, _dot_nt(qs_bf[part[cc], cols[h]], ks_bf[part[cc], cols[h]]), 0.0
                                 ).astype(BF16) for cc, h in pairs}
        intra = {(cc, h): _dot(sc[cc, h], v_bf[part[cc], cols[h]]) for cc, h in pairs}
        state = [state_scr[h] for h in heads]
        inter = {}
        for cc in range(group_chunks):
            for h in heads:
                inter[cc, h] = _dot_nt(qg[cc][:, cols[h]], state[h].astype(BF16))
                state[h] = state[h] * (half[cc][:, cols[h]] * half[cc][:, cols[h]]) + update[cc, h]
        for h in heads:
            state_scr[h] = state[h]
        for cc, h in pairs:
            out_rows = pl.ds(pl.multiple_of(gi * n_rows + cc * HG_CHUNK, HG_CHUNK), HG_CHUNK)
            o_ref[0, out_rows, cols[h]] = (_rms(inter[cc, h] + intra[cc, h], ng_ref[...]) * gated[part[cc], cols[h]]
                                           ).astype(o_ref.dtype)
        return 0

    def chunk_decay(c, worst):
        rows = pl.ds(pl.multiple_of(c * HG_CHUNK, HG_CHUNK), HG_CHUNK)
        log2_f = jnp.log2(lb + (1.0 - lb) * _sigmoid(f_ref[0, rows, :]))
        return jnp.minimum(worst, jnp.sum(log2_f, axis=0, keepdims=True))

    worst = jnp.min(lax.fori_loop(0, nc, chunk_decay, jnp.zeros((1, HG_WIDTH), F32), unroll=HG_UNROLL))
    gentle = worst >= -HG_SAFE_DECAY

    @pl.when(gentle)
    def _():
        lax.fori_loop(0, nc // group_chunks, chunk_group, 0)

    @pl.when(jnp.logical_not(gentle))
    def _():
        lax.fori_loop(0, nc, chunk, 0, unroll=HG_UNROLL)


def _hgrn(proj, f_logit, lb_logits, norm_g, layer):
    b, s, _ = proj.shape
    depth = lb_logits.shape[0]
    assert HG_SUB == 2 * SUBLANES
    blk = lambda off: pl.BlockSpec((1, s, HG_WIDTH), lambda i: (i, 0, off))
    return pl.pallas_call(
        functools.partial(_hgrn_kernel, layer),
        grid=(b,),
        in_specs=[
            pl.BlockSpec((depth, HG_WIDTH), lambda i: (0, 0)),
            pl.BlockSpec((1, HG_DK), lambda i: (0, 0)),
            blk(0), blk(0), blk(2), blk(3),
        ],
        out_specs=pl.BlockSpec((1, s, HG_WIDTH), lambda i: (i, 0, 0)),
        out_shape=jax.ShapeDtypeStruct((b, s, HG_WIDTH), BF16),
        scratch_shapes=[pltpu.VMEM((HG_HEADS, HG_DK, HG_DK), F32)],
        compiler_params=_params(("parallel",)),
        name="hgrn2",
    )(lb_logits, norm_g.reshape(1, HG_DK), proj, f_logit, proj, proj)


def _attn_columns(w_in):
    split = SWA_Q + 2 * SWA_KV
    return jnp.concatenate([w_in[:, split:], w_in[:, :split]], axis=1)


def kernel(x, norm_g, ffn_w_in, ffn_w_out, attn_w_in, attn_sinks, attn_w_out, rec_w_in,
           hgrn_lb_logits, hgrn_norm_g, conv_w, rec_w_out, final_g):
    b, s, d = x.shape
    depth = norm_g.shape[0]
    m = b * s
    h = x.reshape(m, d)
    w_in_bf, w_out_bf = ffn_w_in.astype(BF16), ffn_w_out.astype(BF16)
    for layer in range(depth):
        h = _ffn(h, norm_g[layer, 0], w_in_bf, w_out_bf, layer, 0, final_g, final_norm=False)
        if layer % 2 == 0:
            e = layer // 2
            proj = _proj_in(h, norm_g[layer, 1], _attn_columns(attn_w_in[e]).astype(BF16)).reshape(b, s, ATTN_IN)
            o_a = _swa(proj, attn_sinks[e])
            o_b = _sb(proj)
            mix = (o_a.reshape(m, SWA_Q), o_b.reshape(m, SB_W), attn_w_out[e].astype(BF16))
        else:
            o = layer // 2
            proj, f_logit, o_d = _proj_rec(h, norm_g[layer, 1], rec_w_in[o].astype(BF16), conv_w[o], s)
            o_c = _hgrn(proj.reshape(b, s, 4 * HG_WIDTH), f_logit.reshape(b, s, HG_WIDTH), hgrn_lb_logits,
                        hgrn_norm_g[o], layer)
            mix = (o_c.reshape(m, HG_WIDTH), o_d, rec_w_out[o].astype(BF16))
        h = _ffn(h, norm_g[layer, 2], w_in_bf, w_out_bf, layer, 1, final_g,
                 final_norm=(layer == depth - 1), mix=mix)
    return h.reshape(b, s, d)
```

```python
import functools
import math

import jax
import jax.numpy as jnp
from jax import lax
from jax.experimental import pallas as pl
from jax.experimental.pallas import tpu as pltpu

F32 = jnp.float32
BF16 = jnp.bfloat16

HEAD_DIM = 64
SWA_HEADS = 8
SWA_KV_HEADS = 2
SWA_GROUP = SWA_HEADS // SWA_KV_HEADS
WINDOW = 128
BLK = 128
SUBLANES = 8
SB_HEADS = 8
HG_WIDTH = 512
HG_DK = 128
HG_HEADS = HG_WIDTH // HG_DK
HG_CHUNK = 64
HG_SUB = 16
HG_UNROLL = 4
HG_GROUP = 16
HG_SAFE_DECAY = 200.0
SWA_UNROLL = 8
CONV_WIDTH = 512
CONV_K = 3
EPS = 1e-6

SWA_Q = SWA_HEADS * HEAD_DIM
SWA_KV = SWA_KV_HEADS * HEAD_DIM
SB_W = SB_HEADS * HEAD_DIM
ATTN_IN = SWA_Q + 2 * SWA_KV + 3 * SB_W
REC_IN = 4 * HG_WIDTH + 3 * CONV_WIDTH

VMEM_LIMIT_BYTES = 56 * 1024 * 1024


def _params(semantics):
    return pltpu.CompilerParams(dimension_semantics=semantics,
                                vmem_limit_bytes=VMEM_LIMIT_BYTES)


def _rms(x, g):
    return x * lax.rsqrt(jnp.mean(x * x, axis=-1, keepdims=True) + EPS) * g


def _sigmoid(x):
    return 0.5 * (1.0 + jnp.tanh(0.5 * x))


def _dot(a, b):
    return jnp.dot(a, b, preferred_element_type=F32)


def _dot_nt(a, b):
    return lax.dot_general(a, b, (((1,), (1,)), ((), ())), preferred_element_type=F32)


def _dot_tn(a, b):
    return lax.dot_general(a, b, (((0,), (0,)), ((), ())), preferred_element_type=F32)


def _split_dot_left(m01, x):
    hi = x.astype(BF16)
    lo = (x - hi.astype(F32)).astype(BF16)
    return _dot(m01, hi) + _dot(m01, lo)


def _ffn_kernel(*refs, final_norm, with_mix):
    if with_mix:
        x_ref, a_ref, b_ref, wm_ref, g_ref, wi_ref, wo_ref, fg_ref, o_ref = refs
    else:
        x_ref, g_ref, wi_ref, wo_ref, fg_ref, o_ref = refs
    d_ff = wo_ref.shape[0]
    x = x_ref[...]
    if with_mix:
        ka = a_ref.shape[1]
        x = x + _dot(a_ref[...], wm_ref[:ka, :]) + _dot(b_ref[...], wm_ref[ka:, :])
    half = x.shape[0] // 2
    xs = [x[:half], x[half:]]
    n = [_rms(xi, g_ref[...]).astype(BF16) for xi in xs]
    gate = [_dot(ni, wi_ref[:, :d_ff]) for ni in n]
    up = [_dot(ni, wi_ref[:, d_ff:]) for ni in n]
    act = [(gi * jax.nn.sigmoid(gi) * ui).astype(BF16) for gi, ui in zip(gate, up)]
    hs = [xi + 0.5 * _dot(ai, wo_ref[...]) for xi, ai in zip(xs, act)]
    if final_norm:
        hs = [_rms(hi, fg_ref[...]) for hi in hs]
    o_ref[:half, :] = hs[0]
    o_ref[half:, :] = hs[1]


FFN_TM = 512


def _ffn(h, g, w_in_all, w_out_all, layer, which, final_g, *, final_norm, mix=None):
    m, d = h.shape
    d_ff = w_out_all.shape[2]
    tm = min(FFN_TM, m)
    rows = lambda width: pl.BlockSpec((tm, width), lambda i: (i, 0))
    resident = lambda shape: pl.BlockSpec(shape, lambda i: (0, 0), pipeline_mode=pl.Buffered(1))
    stacked = lambda shape: pl.BlockSpec((None, None) + shape, lambda i: (layer, which, 0, 0),
                                         pipeline_mode=pl.Buffered(1))
    operands, in_specs = [h], [rows(d)]
    if mix is not None:
        a, b, w_mix = mix
        operands += [a, b, w_mix]
        in_specs += [rows(a.shape[1]), rows(b.shape[1]), resident(w_mix.shape)]
    operands += [g.reshape(1, d), w_in_all, w_out_all, final_g.reshape(1, d)]
    in_specs += [resident((1, d)), stacked((d, 2 * d_ff)), stacked((d_ff, d)), resident((1, d))]
    return pl.pallas_call(
        functools.partial(_ffn_kernel, final_norm=final_norm, with_mix=mix is not None),
        grid=(m // tm,),
        in_specs=in_specs,
        out_specs=rows(d),
        out_shape=jax.ShapeDtypeStruct((m, d), F32),
        compiler_params=_params(("parallel",)),
        name="ffn",
    )(*operands)


def _proj_in_kernel(x_ref, g_ref, w_ref, o_ref):
    o_ref[...] = _dot(_rms(x_ref[...], g_ref[...]).astype(BF16), w_ref[...]).astype(o_ref.dtype)


PROJ_TM = 512


def _proj_in(h, g, w):
    m, d = h.shape
    n_out = w.shape[1]
    tm = min(PROJ_TM, m)
    resident = lambda shape: pl.BlockSpec(shape, lambda i: (0, 0), pipeline_mode=pl.Buffered(1))
    return pl.pallas_call(
        _proj_in_kernel,
        grid=(m // tm,),
        in_specs=[pl.BlockSpec((tm, d), lambda i: (i, 0)), resident((1, d)), resident((d, n_out))],
        out_specs=pl.BlockSpec((tm, n_out), lambda i: (i, 0)),
        out_shape=jax.ShapeDtypeStruct((m, n_out), BF16),
        compiler_params=_params(("parallel",)),
        name="proj_in",
    )(h, g.reshape(1, d), w)


def _proj_rec_kernel(x_ref, g_ref, w_ref, cw_ref, o_ref, f_ref, od_ref, tail_scr, *, seq_tiles):
    hg = 4 * HG_WIDTH
    n = _rms(x_ref[...], g_ref[...]).astype(BF16)
    yc = _dot(n, w_ref[:, hg:])
    y = _dot(n, w_ref[:, :hg])
    o_ref[...] = y.astype(o_ref.dtype)
    f_ref[...] = y[:, HG_WIDTH:2 * HG_WIDTH]
    gate_b = yc[:, :CONV_WIDTH]
    x = yc[:, CONV_WIDTH:2 * CONV_WIDTH] * yc[:, 2 * CONV_WIDTH:]
    first = pl.program_id(0) % seq_tiles == 0
    tail = jnp.where(first, 0.0, tail_scr[...])
    row = lax.broadcasted_iota(jnp.int32, x.shape, 0)
    t1 = tail[SUBLANES - 1:SUBLANES, :]
    t2 = tail[SUBLANES - 2:SUBLANES - 1, :]
    x1 = jnp.where(row == 0, t1, pltpu.roll(x, 1, 0))
    x2 = jnp.where(row == 0, t2, jnp.where(row == 1, t1, pltpu.roll(x, 2, 0)))
    conv = cw_ref[0:1, :] * x2 + cw_ref[1:2, :] * x1 + cw_ref[2:3, :] * x
    od_ref[...] = (gate_b * conv).astype(od_ref.dtype)
    tail_scr[...] = x[x.shape[0] - SUBLANES:, :]


def _proj_rec(h, g, w, conv_w, seq_len):
    m, d = h.shape
    tm = min(PROJ_TM, seq_len)
    assert CONV_K - 1 <= SUBLANES and seq_len % tm == 0
    resident = lambda shape: pl.BlockSpec(shape, lambda i: (0, 0), pipeline_mode=pl.Buffered(1))
    rows = lambda width: pl.BlockSpec((tm, width), lambda i: (i, 0))
    return pl.pallas_call(
        functools.partial(_proj_rec_kernel, seq_tiles=seq_len // tm),
        grid=(m // tm,),
        in_specs=[rows(d), resident((1, d)), resident((d, REC_IN)), resident((CONV_K, CONV_WIDTH))],
        out_specs=[rows(4 * HG_WIDTH), rows(HG_WIDTH), rows(CONV_WIDTH)],
        out_shape=[jax.ShapeDtypeStruct((m, 4 * HG_WIDTH), BF16), jax.ShapeDtypeStruct((m, HG_WIDTH), F32),
                   jax.ShapeDtypeStruct((m, CONV_WIDTH), BF16)],
        scratch_shapes=[pltpu.VMEM((SUBLANES, CONV_WIDTH), F32)],
        compiler_params=_params(("arbitrary",)),
        name="proj_rec",
    )(h, g.reshape(1, d), w, conv_w)


def _swa_kernel(sink_ref, q_ref, k_ref, v_ref, o_ref, k_swapped, v_swapped):
    nb = q_ref.shape[1] // BLK
    half = HEAD_DIM
    k_swapped[...] = pltpu.roll(k_ref[0].astype(F32), half, 1).astype(BF16)
    v_swapped[...] = pltpu.roll(v_ref[0].astype(F32), half, 1).astype(BF16)
    qi = lax.broadcasted_iota(jnp.int32, (BLK, 2 * BLK), 0)
    kj = lax.broadcasted_iota(jnp.int32, (BLK, 2 * BLK), 1)
    dist = qi + BLK - kj
    in_window = (dist >= 0) & (dist < WINDOW)
    dist_f = dist.astype(F32)
    lane = lax.broadcasted_iota(jnp.int32, (BLK, BLK), 1)
    scale = HEAD_DIM ** -0.5

    def q_block(n, _):
        cur = pl.ds(pl.multiple_of(n * BLK, BLK), BLK)
        prev = pl.ds(pl.multiple_of(jnp.maximum(n - 1, 0) * BLK, BLK), BLK)
        valid = in_window & (n * BLK - BLK + kj >= 0)
        penalty = jnp.where(valid, dist_f, jnp.inf)
        band = lambda ref: jnp.concatenate([ref[prev, :], ref[cur, :]], axis=0)
        k_by_half = (band(k_ref.at[0]), band(k_swapped))
        v_by_half = (band(v_ref.at[0]), band(v_swapped))
        heads = range(SWA_HEADS)
        kv_half = [(h // SWA_GROUP) ^ (h % 2) for h in heads]
        scores = []
        for h in heads:
            q_col = q_ref[0, cur, (h // 2) * BLK:(h // 2 + 1) * BLK].astype(F32) * scale
            in_half = (lane >= (h % 2) * half) & (lane < (h % 2 + 1) * half)
            qm = jnp.where(in_half, q_col, 0.0).astype(BF16)
            slope = 2.0 ** (-8.0 * (h + 1) / SWA_HEADS)
            scores.append(_dot_nt(qm, k_by_half[kv_half[h]]) - slope * penalty)
        row_max = [jnp.maximum(jnp.max(scores[h], axis=-1, keepdims=True), sink_ref[h]) for h in heads]
        probs = [jnp.exp(scores[h] - row_max[h]) for h in heads]
        num = [_dot(probs[h].astype(BF16), v_by_half[kv_half[h]]) for h in heads]
        den = [jnp.sum(probs[h], axis=-1, keepdims=True) + jnp.exp(sink_ref[h] - row_max[h]) for h in heads]
        out = [num[h] / den[h] for h in heads]
        for col in range(SWA_Q // BLK):
            o_ref[0, cur, col * BLK:(col + 1) * BLK] = jnp.where(
                lane < half, out[2 * col], out[2 * col + 1]).astype(o_ref.dtype)
        return 0

    lax.fori_loop(0, nb, q_block, 0, unroll=SWA_UNROLL)


def _swa(proj, sinks):
    b, s, _ = proj.shape
    assert SWA_KV == BLK and SWA_KV_HEADS == 2
    q_col = 3 * SB_W // SWA_Q
    k_col = (3 * SB_W + SWA_Q) // SWA_KV
    v_col = k_col + 1
    return pl.pallas_call(
        _swa_kernel,
        grid=(b,),
        in_specs=[
            pl.BlockSpec(memory_space=pltpu.SMEM),
            pl.BlockSpec((1, s, SWA_Q), lambda i: (i, 0, q_col)),
            pl.BlockSpec((1, s, SWA_KV), lambda i: (i, 0, k_col)),
            pl.BlockSpec((1, s, SWA_KV), lambda i: (i, 0, v_col)),
        ],
        out_specs=pl.BlockSpec((1, s, SWA_Q), lambda i: (i, 0, 0)),
        out_shape=jax.ShapeDtypeStruct((b, s, SWA_Q), BF16),
        scratch_shapes=[pltpu.VMEM((s, SWA_KV), BF16), pltpu.VMEM((s, SWA_KV), BF16)],
        compiler_params=_params(("parallel",)),
        name="swa",
    )(sinks, proj, proj, proj)


SB_TQ = 256
SB_KPI = 2
LOG2E = math.log2(math.e)
SB_DEAD = -256.0


SB_PAIRS = 4


def _sb_kernel(q_ref, k_ref, v_ref, o_ref, kk_scr, vv_scr, carry_scr, acc_scr):
    s = q_ref.shape[1]
    nt = s // SB_TQ
    nb = s // BLK
    kpt = SB_TQ // BLK
    lane = lax.broadcasted_iota(jnp.int32, (BLK, BLK), 1)
    key_pos = lax.broadcasted_iota(jnp.int32, (SB_TQ, 2 * BLK), 1) & (BLK - 1)
    q_pos = lax.broadcasted_iota(jnp.int32, (SB_TQ, 2 * BLK), 0)
    jj = lax.broadcasted_iota(jnp.int32, (2 * BLK, 2 * BLK), 0) & (BLK - 1)
    ss = lax.broadcasted_iota(jnp.int32, (2 * BLK, 2 * BLK), 1)
    suffix_mat = jnp.where((jj > ss) | (ss >= BLK), 1.0, 0.0).astype(BF16)
    scale = HEAD_DIM ** -0.5

    def prep(j, _):
        rows = pl.ds(pl.multiple_of(j * BLK, BLK), BLK)
        for p in range(SB_PAIRS):
            kb = k_ref[0, rows, p * BLK:(p + 1) * BLK]
            vb = v_ref[0, rows, p * BLK:(p + 1) * BLK]
            kk_scr[p, j] = jnp.concatenate([jnp.where(lane < HEAD_DIM, kb, 0.0),
                                            jnp.where(lane >= HEAD_DIM, kb, 0.0)], axis=0).astype(BF16)
            vv_scr[p, j] = jnp.concatenate([jnp.where(lane < HEAD_DIM, vb, 0.0),
                                            jnp.where(lane >= HEAD_DIM, vb, 0.0)], axis=0).astype(BF16)
        return 0

    lax.fori_loop(0, nb, prep, 0)

    def key_blocks(qn, t, js, masked):
        first_row = [(len(js) - 1 - d) * BLK if masked else 0 for d in range(len(js))]
        chains = [(p, d) for p in range(SB_PAIRS) for d in range(len(js))]
        causal = [((js[d] * BLK + key_pos) < (t * SB_TQ + q_pos))[first_row[d]:] if masked else None
                  for d in range(len(js))]
        nz = {(p, d): _dot_nt(qn[p][first_row[d]:], kk_scr[p, js[d]]) for p, d in chains}
        log_beta, lhs = {}, {}
        for c in chains:
            log_keep = jnp.minimum(nz[c], 0.0) - jnp.log2(1.0 + jnp.exp2(-jnp.abs(nz[c])))
            log_beta[c] = log_keep - nz[c]
            if masked:
                log_keep = jnp.where(causal[c[1]], log_keep, 0.0)
            hi = log_keep.astype(BF16)
            lo = (log_keep - hi.astype(F32)).astype(BF16)
            lhs[c] = [jnp.concatenate([hi[:, h * BLK:(h + 1) * BLK], lo[:, h * BLK:(h + 1) * BLK]], axis=1)
                      for h in range(2)]
        cs = {c: [_dot(lhs[c][h], suffix_mat) for h in range(2)] for c in chains}
        w = {}
        for p in range(SB_PAIRS):
            carry = carry_scr[p]
            for d in range(len(js)):
                c, r0 = (p, d), first_row[d]
                between = jnp.concatenate([cs[c][0][:, :BLK], cs[c][1][:, :BLK]], axis=1)
                wc = jnp.exp2(log_beta[c] + between + carry[r0:])
                if masked:
                    wc = jnp.where(causal[d], wc, 0.0)
                w[c] = wc.astype(BF16)
                below = carry[r0:] + jnp.concatenate([cs[c][0][:, BLK:], cs[c][1][:, BLK:]], axis=1)
                carry = below if r0 == 0 else jnp.concatenate([carry[:r0], below], axis=0)
            carry_scr[p] = carry
        for p in range(SB_PAIRS):
            acc = acc_scr[p]
            for d, j in enumerate(js):
                r0 = first_row[d]
                below = acc[r0:] + _dot(w[(p, d)], vv_scr[p, j])
                acc = below if r0 == 0 else jnp.concatenate([acc[:r0], below], axis=0)
            acc_scr[p] = acc

    def q_tile(t, _):
        rows = pl.ds(pl.multiple_of(t * SB_TQ, SB_TQ), SB_TQ)
        qn = [(q_ref[0, rows, p * BLK:(p + 1) * BLK].astype(F32) * (-scale * LOG2E)).astype(BF16)
              for p in range(SB_PAIRS)]
        carry_scr[...] = jnp.zeros_like(carry_scr)
        acc_scr[...] = jnp.zeros_like(acc_scr)
        key_blocks(qn, t, [kpt * t + kpt - 1 - d for d in range(kpt)], True)

        def live(state):
            return (state[0] < kpt * t // SB_KPI) & (state[1] > SB_DEAD)

        def kv_step(state):
            i = state[0]
            key_blocks(qn, t, [kpt * t - 1 - i * SB_KPI - d for d in range(SB_KPI)], False)
            return i + 1, jnp.max(carry_scr[...])

        lax.while_loop(live, kv_step, (jnp.int32(0), jnp.float32(0.0)))
        for p in range(SB_PAIRS):
            o_ref[0, rows, p * BLK:(p + 1) * BLK] = acc_scr[p].astype(o_ref.dtype)
        return 0

    lax.fori_loop(0, nt, q_tile, 0)


def _sb(proj):
    b, s, _ = proj.shape
    assert (SB_TQ // BLK) % SB_KPI == 0
    width = SB_PAIRS * BLK
    q_col = 0
    k_col = SB_W // width
    v_col = 2 * SB_W // width
    return pl.pallas_call(
        _sb_kernel,
        grid=(b, SB_W // width),
        in_specs=[
            pl.BlockSpec((1, s, width), lambda i, p: (i, 0, q_col + p)),
            pl.BlockSpec((1, s, width), lambda i, p: (i, 0, k_col + p)),
            pl.BlockSpec((1, s, width), lambda i, p: (i, 0, v_col + p)),
        ],
        out_specs=pl.BlockSpec((1, s, width), lambda i, p: (i, 0, p)),
        out_shape=jax.ShapeDtypeStruct((b, s, SB_W), BF16),
        scratch_shapes=[pltpu.VMEM((SB_PAIRS, s // BLK, 2 * BLK, BLK), BF16),
                        pltpu.VMEM((SB_PAIRS, s // BLK, 2 * BLK, BLK), BF16),
                        pltpu.VMEM((SB_PAIRS, SB_TQ, 2 * BLK), F32), pltpu.VMEM((SB_PAIRS, SB_TQ, BLK), F32)],
        compiler_params=_params(("parallel", "parallel")),
        name="stick_breaking",
    )(proj, proj, proj)


def _hgrn_kernel(layer, lbl_ref, ng_ref, q_ref, f_ref, i_ref, gate_ref, o_ref, state_scr):
    nc = q_ref.shape[1] // HG_CHUNK
    group_chunks = math.gcd(nc, HG_GROUP)
    logits = lbl_ref[...]
    e = jnp.exp(logits - jnp.max(logits, axis=0, keepdims=True))
    sm = e / jnp.sum(e, axis=0, keepdims=True)
    lb = jnp.sum(sm[1:layer + 1], axis=0, keepdims=True)
    ti = lax.broadcasted_iota(jnp.int32, (HG_CHUNK, HG_CHUNK), 0)
    si = lax.broadcasted_iota(jnp.int32, (HG_CHUNK, HG_CHUNK), 1)
    cumsum_mat = jnp.where(si <= ti, 1.0, 0.0).astype(BF16)
    row8 = lax.broadcasted_iota(jnp.int32, (SUBLANES, 1), 0)
    n_sub = HG_CHUNK // HG_SUB
    state_scr[...] = jnp.zeros_like(state_scr)

    def pair_rows(acc, qt, gt, ks, gs, vs, s):
        decay = jnp.exp2(gt - gs[s:s + 1, :])
        col = jnp.sum(qt * (ks[s:s + 1, :] * decay), axis=-1, keepdims=True)
        col = jnp.where(row8 >= s, col, 0.0)
        return acc + col * vs[s:s + 1, :]

    def chunk(c, _):
        rows = pl.ds(pl.multiple_of(c * HG_CHUNK, HG_CHUNK), HG_CHUNK)
        z = f_ref[0, rows, :]
        qx = q_ref[0, rows, :].astype(F32)
        v_all = i_ref[0, rows, :].astype(F32)
        gate = gate_ref[0, rows, :].astype(F32)
        sig = _sigmoid(z)
        q_all = qx * _sigmoid(qx)
        k_all = (1.0 - lb) * (1.0 - sig)
        log2_f = jnp.log2(lb + (1.0 - lb) * sig)
        g_all = _split_dot_left(cumsum_mat, log2_f)
        g_last_all = g_all[HG_CHUNK - 1:HG_CHUNK, :]
        gated = gate * _sigmoid(gate)
        heads = range(HG_HEADS)
        cols = [slice(h * HG_DK, (h + 1) * HG_DK) for h in heads]
        v_bf = v_all.astype(BF16)
        state = [state_scr[h] for h in heads]
        qg_all = (q_all * jnp.exp2(g_all)).astype(BF16)
        kd_all = (k_all * jnp.exp2(g_last_all - g_all)).astype(BF16)
        o_inter = [_dot_nt(qg_all[:, cols[h]], state[h].astype(BF16)) for h in heads]
        new_state = [state[h] * jnp.exp2(g_last_all[:, cols[h]]) + _dot_tn(v_bf[:, cols[h]], kd_all[:, cols[h]])
                     for h in heads]
        kt_all, qt_all = [], []
        for j in range(n_sub - 1):
            lo, hi = j * HG_SUB, (j + 1) * HG_SUB
            r = g_all[hi - 1:hi, :]
            kt_all.append((k_all[lo:hi] * jnp.exp2(r - g_all[lo:hi])).astype(BF16))
            qt_all.append((q_all[hi:] * jnp.exp2(g_all[hi:] - r)).astype(BF16))
        q_low, k_up, v_up = [], [], []
        for i in range(n_sub):
            lo, mid, hi = i * HG_SUB, i * HG_SUB + SUBLANES, (i + 1) * HG_SUB
            r = g_all[mid - 1:mid, :]
            q_low.append(q_all[mid:hi] * jnp.exp2(g_all[mid:hi] - r))
            k_up.append(k_all[lo:mid] * jnp.exp2(r - g_all[lo:mid]))
            v_up.append(v_all[lo:mid])
        q_low = jnp.concatenate(q_low, axis=0).astype(BF16)
        k_up = jnp.concatenate(k_up, axis=0).astype(BF16)
        v_up = jnp.concatenate(v_up, axis=0).astype(BF16)
        same_sub = (lax.broadcasted_iota(jnp.int32, (n_sub * SUBLANES, n_sub * SUBLANES), 0) // SUBLANES
                    == lax.broadcasted_iota(jnp.int32, (n_sub * SUBLANES, n_sub * SUBLANES), 1) // SUBLANES)
        sc = [[_dot_nt(qt_all[j][:, cols[h]], kt_all[j][:, cols[h]]).astype(BF16)
               for j in range(n_sub - 1)] for h in heads]
        contrib = [[_dot(sc[h][j], v_bf[j * HG_SUB:(j + 1) * HG_SUB, cols[h]])
                    for j in range(n_sub - 1)] for h in heads]
        sc_low = [jnp.where(same_sub, _dot_nt(q_low[:, cols[h]], k_up[:, cols[h]]), 0.0).astype(BF16) for h in heads]
        contrib_low = [_dot(sc_low[h], v_up[:, cols[h]]) for h in heads]
        tiles = [[None] * (HG_CHUNK // SUBLANES) for _ in heads]
        for h in heads:
            for i in range(n_sub):
                part = jnp.zeros((HG_SUB, HG_DK), F32)
                for j in range(i):
                    part = part + contrib[h][j][(i - j - 1) * HG_SUB:(i - j) * HG_SUB]
                tiles[h][2 * i] = part[:SUBLANES]
                tiles[h][2 * i + 1] = part[SUBLANES:] + contrib_low[h][i * SUBLANES:(i + 1) * SUBLANES]
        for i in range(2 * n_sub):
            a = slice(i * SUBLANES, (i + 1) * SUBLANES)
            for s in range(SUBLANES):
                for h in heads:
                    g, q, k, v = g_all[:, cols[h]], q_all[:, cols[h]], k_all[:, cols[h]], v_all[:, cols[h]]
                    tiles[h][i] = pair_rows(tiles[h][i], q[a], g[a], k[a], g[a], v[a], s)
        for h in heads:
            o = o_inter[h] + jnp.concatenate(tiles[h], axis=0)
            state_scr[h] = new_state[h]
            o_ref[0, rows, cols[h]] = (_rms(o, ng_ref[...]) * gated[:, cols[h]]).astype(o_ref.dtype)
        return 0

    def chunk_group(gi, _):
        n_rows = group_chunks * HG_CHUNK
        rows = pl.ds(pl.multiple_of(gi * n_rows, n_rows), n_rows)
        part = [slice(cc * HG_CHUNK, (cc + 1) * HG_CHUNK) for cc in range(group_chunks)]
        heads = range(HG_HEADS)
        cols = [slice(h * HG_DK, (h + 1) * HG_DK) for h in heads]
        pairs = [(cc, h) for cc in range(group_chunks) for h in heads]
        z = f_ref[0, rows, :]
        qx = q_ref[0, rows, :].astype(F32)
        v_bf = i_ref[0, rows, :]
        gate = gate_ref[0, rows, :].astype(F32)
        sig = _sigmoid(z)
        q_all = qx * _sigmoid(qx)
        k_all = (1.0 - lb) * (1.0 - sig)
        log2_f = jnp.log2(lb + (1.0 - lb) * sig)
        gated = gate * _sigmoid(gate)
        hi = log2_f.astype(BF16)
        lo = (log2_f - hi.astype(F32)).astype(BF16)
        g = [_dot(cumsum_mat, hi[p]) + _dot(cumsum_mat, lo[p]) for p in part]
        r = [0.5 * gc[HG_CHUNK - 1:HG_CHUNK, :] for gc in g]
        half = [jnp.exp2(rc) for rc in r]
        centred = jnp.concatenate([gc - rc for gc, rc in zip(g, r)], axis=0)
        qs = q_all * jnp.exp2(centred)
        ks = k_all * jnp.exp2(-centred)
        qs_bf, ks_bf = qs.astype(BF16), ks.astype(BF16)
        qg = [(qs[p] * hc).astype(BF16) for p, hc in zip(part, half)]
        kd = [(ks[p] * hc).astype(BF16) for p, hc in zip(part, half)]
        update = {(cc, h): _dot_tn(v_bf[part[cc], cols[h]], kd[cc][:, cols[h]]) for cc, h in pairs}
        sc = {(cc, h): jnp.where(si <= ti, _dot_nt(qs_bf[part[cc], cols[h]], ks_bf[part[cc], cols[h]]), 0.0
                                 ).astype(BF16) for cc, h in pairs}
        intra = {(cc, h): _dot(sc[cc, h], v_bf[part[cc], cols[h]]) for cc, h in pairs}
        state = [state_scr[h] for h in heads]
        inter = {}
        for cc in range(group_chunks):
            for h in heads:
                inter[cc, h] = _dot_nt(qg[cc][:, cols[h]], state[h].astype(BF16))
                state[h] = state[h] * (half[cc][:, cols[h]] * half[cc][:, cols[h]]) + update[cc, h]
        for h in heads:
            state_scr[h] = state[h]
        for cc, h in pairs:
            out_rows = pl.ds(pl.multiple_of(gi * n_rows + cc * HG_CHUNK, HG_CHUNK), HG_CHUNK)
            o_ref[0, out_rows, cols[h]] = (_rms(inter[cc, h] + intra[cc, h], ng_ref[...]) * gated[part[cc], cols[h]]
                                           ).astype(o_ref.dtype)
        return 0

    def chunk_decay(c, worst):
        rows = pl.ds(pl.multiple_of(c * HG_CHUNK, HG_CHUNK), HG_CHUNK)
        log2_f = jnp.log2(lb + (1.0 - lb) * _sigmoid(f_ref[0, rows, :]))
        return jnp.minimum(worst, jnp.sum(log2_f, axis=0, keepdims=True))

    worst = jnp.min(lax.fori_loop(0, nc, chunk_decay, jnp.zeros((1, HG_WIDTH), F32), unroll=HG_UNROLL))
    gentle = worst >= -HG_SAFE_DECAY

    @pl.when(gentle)
    def _():
        lax.fori_loop(0, nc // group_chunks, chunk_group, 0)

    @pl.when(jnp.logical_not(gentle))
    def _():
        lax.fori_loop(0, nc, chunk, 0, unroll=HG_UNROLL)


def _hgrn(proj, f_logit, lb_logits, norm_g, layer):
    b, s, _ = proj.shape
    depth = lb_logits.shape[0]
    assert HG_SUB == 2 * SUBLANES
    blk = lambda off: pl.BlockSpec((1, s, HG_WIDTH), lambda i: (i, 0, off))
    return pl.pallas_call(
        functools.partial(_hgrn_kernel, layer),
        grid=(b,),
        in_specs=[
            pl.BlockSpec((depth, HG_WIDTH), lambda i: (0, 0)),
            pl.BlockSpec((1, HG_DK), lambda i: (0, 0)),
            blk(0), blk(0), blk(2), blk(3),
        ],
        out_specs=pl.BlockSpec((1, s, HG_WIDTH), lambda i: (i, 0, 0)),
        out_shape=jax.ShapeDtypeStruct((b, s, HG_WIDTH), BF16),
        scratch_shapes=[pltpu.VMEM((HG_HEADS, HG_DK, HG_DK), F32)],
        compiler_params=_params(("parallel",)),
        name="hgrn2",
    )(lb_logits, norm_g.reshape(1, HG_DK), proj, f_logit, proj, proj)


def _attn_columns(w_in):
    split = SWA_Q + 2 * SWA_KV
    return jnp.concatenate([w_in[:, split:], w_in[:, :split]], axis=1)


def kernel(x, norm_g, ffn_w_in, ffn_w_out, attn_w_in, attn_sinks, attn_w_out, rec_w_in,
           hgrn_lb_logits, hgrn_norm_g, conv_w, rec_w_out, final_g):
    b, s, d = x.shape
    depth = norm_g.shape[0]
    m = b * s
    h = x.reshape(m, d)
    w_in_bf, w_out_bf = ffn_w_in.astype(BF16), ffn_w_out.astype(BF16)
    for layer in range(depth):
        h = _ffn(h, norm_g[layer, 0], w_in_bf, w_out_bf, layer, 0, final_g, final_norm=False)
        if layer % 2 == 0:
            e = layer // 2
            proj = _proj_in(h, norm_g[layer, 1], _attn_columns(attn_w_in[e]).astype(BF16)).reshape(b, s, ATTN_IN)
            o_a = _swa(proj, attn_sinks[e])
            o_b = _sb(proj)
            mix = (o_a.reshape(m, SWA_Q), o_b.reshape(m, SB_W), attn_w_out[e].astype(BF16))
        else:
            o = layer // 2
            proj, f_logit, o_d = _proj_rec(h, norm_g[layer, 1], rec_w_in[o].astype(BF16), conv_w[o], s)
            o_c = _hgrn(proj.reshape(b, s, 4 * HG_WIDTH), f_logit.reshape(b, s, HG_WIDTH), hgrn_lb_logits,
                        hgrn_norm_g[o], layer)
            mix = (o_c.reshape(m, HG_WIDTH), o_d, rec_w_out[o].astype(BF16))
        h = _ffn(h, norm_g[layer, 2], w_in_bf, w_out_bf, layer, 1, final_g,
                 final_norm=(layer == depth - 1), mix=mix)
    return h.reshape(b, s, d)
```

```python
import functools
import math

import jax
import jax.numpy as jnp
from jax import lax
from jax.experimental import pallas as pl
from jax.experimental.pallas import tpu as pltpu

F32 = jnp.float32
BF16 = jnp.bfloat16

HEAD_DIM = 64
SWA_HEADS = 8
SWA_KV_HEADS = 2
SWA_GROUP = SWA_HEADS // SWA_KV_HEADS
WINDOW = 128
BLK = 128
SUBLANES = 8
SB_HEADS = 8
HG_WIDTH = 512
HG_DK = 128
HG_HEADS = HG_WIDTH // HG_DK
HG_CHUNK = 64
HG_SUB = 16
HG_UNROLL = 4
HG_GROUP = 16
HG_SAFE_DECAY = 200.0
SWA_UNROLL = 8
CONV_WIDTH = 512
CONV_K = 3
EPS = 1e-6

SWA_Q = SWA_HEADS * HEAD_DIM
SWA_KV = SWA_KV_HEADS * HEAD_DIM
SB_W = SB_HEADS * HEAD_DIM
ATTN_IN = SWA_Q + 2 * SWA_KV + 3 * SB_W
REC_IN = 4 * HG_WIDTH + 3 * CONV_WIDTH

VMEM_LIMIT_BYTES = 56 * 1024 * 1024


def _params(semantics):
    return pltpu.CompilerParams(dimension_semantics=semantics,
                                vmem_limit_bytes=VMEM_LIMIT_BYTES)


def _rms(x, g):
    return x * lax.rsqrt(jnp.mean(x * x, axis=-1, keepdims=True) + EPS) * g


def _sigmoid(x):
    return 0.5 * (1.0 + jnp.tanh(0.5 * x))


def _dot(a, b):
    return jnp.dot(a, b, preferred_element_type=F32)


def _dot_nt(a, b):
    return lax.dot_general(a, b, (((1,), (1,)), ((), ())), preferred_element_type=F32)


def _dot_tn(a, b):
    return lax.dot_general(a, b, (((0,), (0,)), ((), ())), preferred_element_type=F32)


def _split_dot_left(m01, x):
    hi = x.astype(BF16)
    lo = (x - hi.astype(F32)).astype(BF16)
    return _dot(m01, hi) + _dot(m01, lo)


def _ffn_kernel(*refs, final_norm, with_mix):
    if with_mix:
        x_ref, a_ref, b_ref, wm_ref, g_ref, wi_ref, wo_ref, fg_ref, o_ref = refs
    else:
        x_ref, g_ref, wi_ref, wo_ref, fg_ref, o_ref = refs
    d_ff = wo_ref.shape[0]
    x = x_ref[...]
    if with_mix:
        ka = a_ref.shape[1]
        x = x + _dot(a_ref[...], wm_ref[:ka, :]) + _dot(b_ref[...], wm_ref[ka:, :])
    half = x.shape[0] // 2
    xs = [x[:half], x[half:]]
    n = [_rms(xi, g_ref[...]).astype(BF16) for xi in xs]
    gate = [_dot(ni, wi_ref[:, :d_ff]) for ni in n]
    up = [_dot(ni, wi_ref[:, d_ff:]) for ni in n]
    act = [(gi * jax.nn.sigmoid(gi) * ui).astype(BF16) for gi, ui in zip(gate, up)]
    hs = [xi + 0.5 * _dot(ai, wo_ref[...]) for xi, ai in zip(xs, act)]
    if final_norm:
        hs = [_rms(hi, fg_ref[...]) for hi in hs]
    o_ref[:half, :] = hs[0]
    o_ref[half:, :] = hs[1]


FFN_TM = 512


def _ffn(h, g, w_in_all, w_out_all, layer, which, final_g, *, final_norm, mix=None):
    m, d = h.shape
    d_ff = w_out_all.shape[2]
    tm = min(FFN_TM, m)
    rows = lambda width: pl.BlockSpec((tm, width), lambda i: (i, 0))
    resident = lambda shape: pl.BlockSpec(shape, lambda i: (0, 0), pipeline_mode=pl.Buffered(1))
    stacked = lambda shape: pl.BlockSpec((None, None) + shape, lambda i: (layer, which, 0, 0),
                                         pipeline_mode=pl.Buffered(1))
    operands, in_specs = [h], [rows(d)]
    if mix is not None:
        a, b, w_mix = mix
        operands += [a, b, w_mix]
        in_specs += [rows(a.shape[1]), rows(b.shape[1]), resident(w_mix.shape)]
    operands += [g.reshape(1, d), w_in_all, w_out_all, final_g.reshape(1, d)]
    in_specs += [resident((1, d)), stacked((d, 2 * d_ff)), stacked((d_ff, d)), resident((1, d))]
    return pl.pallas_call(
        functools.partial(_ffn_kernel, final_norm=final_norm, with_mix=mix is not None),
        grid=(m // tm,),
        in_specs=in_specs,
        out_specs=rows(d),
        out_shape=jax.ShapeDtypeStruct((m, d), F32),
        compiler_params=_params(("parallel",)),
        name="ffn",
    )(*operands)


def _proj_in_kernel(x_ref, g_ref, w_ref, o_ref):
    half = x_ref.shape[0] // 2
    n = [_rms(x_ref[rows, :], g_ref[...]).astype(BF16) for rows in (slice(0, half), slice(half, 2 * half))]
    y = [_dot(ni, w_ref[...]) for ni in n]
    o_ref[:half, :] = y[0].astype(o_ref.dtype)
    o_ref[half:, :] = y[1].astype(o_ref.dtype)


PROJ_TM = 512


def _proj_in(h, g, w):
    m, d = h.shape
    n_out = w.shape[1]
    tm = min(PROJ_TM, m)
    resident = lambda shape: pl.BlockSpec(shape, lambda i: (0, 0), pipeline_mode=pl.Buffered(1))
    return pl.pallas_call(
        _proj_in_kernel,
        grid=(m // tm,),
        in_specs=[pl.BlockSpec((tm, d), lambda i: (i, 0)), resident((1, d)), resident((d, n_out))],
        out_specs=pl.BlockSpec((tm, n_out), lambda i: (i, 0)),
        out_shape=jax.ShapeDtypeStruct((m, n_out), BF16),
        compiler_params=_params(("parallel",)),
        name="proj_in",
    )(h, g.reshape(1, d), w)


def _proj_rec_kernel(x_ref, g_ref, w_ref, cw_ref, o_ref, f_ref, od_ref, tail_scr, *, seq_tiles):
    y = _dot(_rms(x_ref[...], g_ref[...]).astype(BF16), w_ref[...])
    hg = 4 * HG_WIDTH
    o_ref[...] = y[:, :hg].astype(o_ref.dtype)
    f_ref[...] = y[:, HG_WIDTH:2 * HG_WIDTH]
    gate_b = y[:, hg:hg + CONV_WIDTH]
    x = y[:, hg + CONV_WIDTH:hg + 2 * CONV_WIDTH] * y[:, hg + 2 * CONV_WIDTH:]
    first = pl.program_id(0) % seq_tiles == 0
    tail = jnp.where(first, 0.0, tail_scr[...])
    row = lax.broadcasted_iota(jnp.int32, x.shape, 0)
    t1 = tail[SUBLANES - 1:SUBLANES, :]
    t2 = tail[SUBLANES - 2:SUBLANES - 1, :]
    x1 = jnp.where(row == 0, t1, pltpu.roll(x, 1, 0))
    x2 = jnp.where(row == 0, t2, jnp.where(row == 1, t1, pltpu.roll(x, 2, 0)))
    conv = cw_ref[0:1, :] * x2 + cw_ref[1:2, :] * x1 + cw_ref[2:3, :] * x
    od_ref[...] = (gate_b * conv).astype(od_ref.dtype)
    tail_scr[...] = x[x.shape[0] - SUBLANES:, :]


def _proj_rec(h, g, w, conv_w, seq_len):
    m, d = h.shape
    tm = min(PROJ_TM, seq_len)
    assert CONV_K - 1 <= SUBLANES and seq_len % tm == 0
    resident = lambda shape: pl.BlockSpec(shape, lambda i: (0, 0), pipeline_mode=pl.Buffered(1))
    rows = lambda width: pl.BlockSpec((tm, width), lambda i: (i, 0))
    return pl.pallas_call(
        functools.partial(_proj_rec_kernel, seq_tiles=seq_len // tm),
        grid=(m // tm,),
        in_specs=[rows(d), resident((1, d)), resident((d, REC_IN)), resident((CONV_K, CONV_WIDTH))],
        out_specs=[rows(4 * HG_WIDTH), rows(HG_WIDTH), rows(CONV_WIDTH)],
        out_shape=[jax.ShapeDtypeStruct((m, 4 * HG_WIDTH), BF16), jax.ShapeDtypeStruct((m, HG_WIDTH), F32),
                   jax.ShapeDtypeStruct((m, CONV_WIDTH), BF16)],
        scratch_shapes=[pltpu.VMEM((SUBLANES, CONV_WIDTH), F32)],
        compiler_params=_params(("arbitrary",)),
        name="proj_rec",
    )(h, g.reshape(1, d), w, conv_w)


def _swa_kernel(sink_ref, q_ref, k_ref, v_ref, o_ref, k_swapped, v_swapped):
    nb = q_ref.shape[1] // BLK
    half = HEAD_DIM
    k_swapped[...] = pltpu.roll(k_ref[0].astype(F32), half, 1).astype(BF16)
    v_swapped[...] = pltpu.roll(v_ref[0].astype(F32), half, 1).astype(BF16)
    qi = lax.broadcasted_iota(jnp.int32, (BLK, 2 * BLK), 0)
    kj = lax.broadcasted_iota(jnp.int32, (BLK, 2 * BLK), 1)
    dist = qi + BLK - kj
    in_window = (dist >= 0) & (dist < WINDOW)
    dist_f = dist.astype(F32)
    lane = lax.broadcasted_iota(jnp.int32, (BLK, BLK), 1)
    scale = HEAD_DIM ** -0.5

    def q_block(n, _):
        cur = pl.ds(pl.multiple_of(n * BLK, BLK), BLK)
        prev = pl.ds(pl.multiple_of(jnp.maximum(n - 1, 0) * BLK, BLK), BLK)
        valid = in_window & (n * BLK - BLK + kj >= 0)
        penalty = jnp.where(valid, dist_f, jnp.inf)
        band = lambda ref: jnp.concatenate([ref[prev, :], ref[cur, :]], axis=0)
        k_by_half = (band(k_ref.at[0]), band(k_swapped))
        v_by_half = (band(v_ref.at[0]), band(v_swapped))
        heads = range(SWA_HEADS)
        kv_half = [(h // SWA_GROUP) ^ (h % 2) for h in heads]
        scores = []
        for h in heads:
            q_col = q_ref[0, cur, (h // 2) * BLK:(h // 2 + 1) * BLK].astype(F32) * scale
            in_half = (lane >= (h % 2) * half) & (lane < (h % 2 + 1) * half)
            qm = jnp.where(in_half, q_col, 0.0).astype(BF16)
            slope = 2.0 ** (-8.0 * (h + 1) / SWA_HEADS)
            scores.append(_dot_nt(qm, k_by_half[kv_half[h]]) - slope * penalty)
        row_max = [jnp.maximum(jnp.max(scores[h], axis=-1, keepdims=True), sink_ref[h]) for h in heads]
        probs = [jnp.exp(scores[h] - row_max[h]) for h in heads]
        num = [_dot(probs[h].astype(BF16), v_by_half[kv_half[h]]) for h in heads]
        den = [jnp.sum(probs[h], axis=-1, keepdims=True) + jnp.exp(sink_ref[h] - row_max[h]) for h in heads]
        out = [num[h] / den[h] for h in heads]
        for col in range(SWA_Q // BLK):
            o_ref[0, cur, col * BLK:(col + 1) * BLK] = jnp.where(
                lane < half, out[2 * col], out[2 * col + 1]).astype(o_ref.dtype)
        return 0

    lax.fori_loop(0, nb, q_block, 0, unroll=SWA_UNROLL)


def _swa(proj, sinks):
    b, s, _ = proj.shape
    assert SWA_KV == BLK and SWA_KV_HEADS == 2
    q_col = 3 * SB_W // SWA_Q
    k_col = (3 * SB_W + SWA_Q) // SWA_KV
    v_col = k_col + 1
    return pl.pallas_call(
        _swa_kernel,
        grid=(b,),
        in_specs=[
            pl.BlockSpec(memory_space=pltpu.SMEM),
            pl.BlockSpec((1, s, SWA_Q), lambda i: (i, 0, q_col)),
            pl.BlockSpec((1, s, SWA_KV), lambda i: (i, 0, k_col)),
            pl.BlockSpec((1, s, SWA_KV), lambda i: (i, 0, v_col)),
        ],
        out_specs=pl.BlockSpec((1, s, SWA_Q), lambda i: (i, 0, 0)),
        out_shape=jax.ShapeDtypeStruct((b, s, SWA_Q), BF16),
        scratch_shapes=[pltpu.VMEM((s, SWA_KV), BF16), pltpu.VMEM((s, SWA_KV), BF16)],
        compiler_params=_params(("parallel",)),
        name="swa",
    )(sinks, proj, proj, proj)


SB_TQ = 256
SB_KPI = 2
LOG2E = math.log2(math.e)
SB_DEAD = -256.0


SB_PAIRS = 4


def _sb_kernel(q_ref, k_ref, v_ref, o_ref, kk_scr, vv_scr, carry_scr, acc_scr):
    s = q_ref.shape[1]
    nt = s // SB_TQ
    nb = s // BLK
    kpt = SB_TQ // BLK
    lane = lax.broadcasted_iota(jnp.int32, (BLK, BLK), 1)
    key_pos = lax.broadcasted_iota(jnp.int32, (SB_TQ, 2 * BLK), 1) & (BLK - 1)
    q_pos = lax.broadcasted_iota(jnp.int32, (SB_TQ, 2 * BLK), 0)
    jj = lax.broadcasted_iota(jnp.int32, (2 * BLK, 2 * BLK), 0) & (BLK - 1)
    ss = lax.broadcasted_iota(jnp.int32, (2 * BLK, 2 * BLK), 1)
    suffix_mat = jnp.where((jj > ss) | (ss >= BLK), 1.0, 0.0).astype(BF16)
    scale = HEAD_DIM ** -0.5

    def prep(j, _):
        rows = pl.ds(pl.multiple_of(j * BLK, BLK), BLK)
        for p in range(SB_PAIRS):
            kb = k_ref[0, rows, p * BLK:(p + 1) * BLK]
            vb = v_ref[0, rows, p * BLK:(p + 1) * BLK]
            kk_scr[p, j] = jnp.concatenate([jnp.where(lane < HEAD_DIM, kb, 0.0),
                                            jnp.where(lane >= HEAD_DIM, kb, 0.0)], axis=0).astype(BF16)
            vv_scr[p, j] = jnp.concatenate([jnp.where(lane < HEAD_DIM, vb, 0.0),
                                            jnp.where(lane >= HEAD_DIM, vb, 0.0)], axis=0).astype(BF16)
        return 0

    lax.fori_loop(0, nb, prep, 0)

    def key_blocks(qn, t, js, masked):
        first_row = [(len(js) - 1 - d) * BLK if masked else 0 for d in range(len(js))]
        chains = [(p, d) for p in range(SB_PAIRS) for d in range(len(js))]
        causal = [((js[d] * BLK + key_pos) < (t * SB_TQ + q_pos))[first_row[d]:] if masked else None
                  for d in range(len(js))]
        nz = {(p, d): _dot_nt(qn[p][first_row[d]:], kk_scr[p, js[d]]) for p, d in chains}
        log_beta, lhs = {}, {}
        for c in chains:
            log_keep = jnp.minimum(nz[c], 0.0) - jnp.log2(1.0 + jnp.exp2(-jnp.abs(nz[c])))
            log_beta[c] = log_keep - nz[c]
            if masked:
                log_keep = jnp.where(causal[c[1]], log_keep, 0.0)
            hi = log_keep.astype(BF16)
            lo = (log_keep - hi.astype(F32)).astype(BF16)
            lhs[c] = [jnp.concatenate([hi[:, h * BLK:(h + 1) * BLK], lo[:, h * BLK:(h + 1) * BLK]], axis=1)
                      for h in range(2)]
        cs = {c: [_dot(lhs[c][h], suffix_mat) for h in range(2)] for c in chains}
        w = {}
        for p in range(SB_PAIRS):
            carry = carry_scr[p]
            for d in range(len(js)):
                c, r0 = (p, d), first_row[d]
                between = jnp.concatenate([cs[c][0][:, :BLK], cs[c][1][:, :BLK]], axis=1)
                wc = jnp.exp2(log_beta[c] + between + carry[r0:])
                if masked:
                    wc = jnp.where(causal[d], wc, 0.0)
                w[c] = wc.astype(BF16)
                below = carry[r0:] + jnp.concatenate([cs[c][0][:, BLK:], cs[c][1][:, BLK:]], axis=1)
                carry = below if r0 == 0 else jnp.concatenate([carry[:r0], below], axis=0)
            carry_scr[p] = carry
        for p in range(SB_PAIRS):
            acc = acc_scr[p]
            for d, j in enumerate(js):
                r0 = first_row[d]
                below = acc[r0:] + _dot(w[(p, d)], vv_scr[p, j])
                acc = below if r0 == 0 else jnp.concatenate([acc[:r0], below], axis=0)
            acc_scr[p] = acc

    def q_tile(t, _):
        rows = pl.ds(pl.multiple_of(t * SB_TQ, SB_TQ), SB_TQ)
        qn = [(q_ref[0, rows, p * BLK:(p + 1) * BLK].astype(F32) * (-scale * LOG2E)).astype(BF16)
              for p in range(SB_PAIRS)]
        carry_scr[...] = jnp.zeros_like(carry_scr)
        acc_scr[...] = jnp.zeros_like(acc_scr)
        key_blocks(qn, t, [kpt * t + kpt - 1 - d for d in range(kpt)], True)

        def live(state):
            return (state[0] < kpt * t // SB_KPI) & (state[1] > SB_DEAD)

        def kv_step(state):
            i = state[0]
            key_blocks(qn, t, [kpt * t - 1 - i * SB_KPI - d for d in range(SB_KPI)], False)
            return i + 1, jnp.max(carry_scr[...])

        lax.while_loop(live, kv_step, (jnp.int32(0), jnp.float32(0.0)))
        for p in range(SB_PAIRS):
            o_ref[0, rows, p * BLK:(p + 1) * BLK] = acc_scr[p].astype(o_ref.dtype)
        return 0

    lax.fori_loop(0, nt, q_tile, 0)


def _sb(proj):
    b, s, _ = proj.shape
    assert (SB_TQ // BLK) % SB_KPI == 0
    width = SB_PAIRS * BLK
    q_col = 0
    k_col = SB_W // width
    v_col = 2 * SB_W // width
    return pl.pallas_call(
        _sb_kernel,
        grid=(b, SB_W // width),
        in_specs=[
            pl.BlockSpec((1, s, width), lambda i, p: (i, 0, q_col + p)),
            pl.BlockSpec((1, s, width), lambda i, p: (i, 0, k_col + p)),
            pl.BlockSpec((1, s, width), lambda i, p: (i, 0, v_col + p)),
        ],
        out_specs=pl.BlockSpec((1, s, width), lambda i, p: (i, 0, p)),
        out_shape=jax.ShapeDtypeStruct((b, s, SB_W), BF16),
        scratch_shapes=[pltpu.VMEM((SB_PAIRS, s // BLK, 2 * BLK, BLK), BF16),
                        pltpu.VMEM((SB_PAIRS, s // BLK, 2 * BLK, BLK), BF16),
                        pltpu.VMEM((SB_PAIRS, SB_TQ, 2 * BLK), F32), pltpu.VMEM((SB_PAIRS, SB_TQ, BLK), F32)],
        compiler_params=_params(("parallel", "parallel")),
        name="stick_breaking",
    )(proj, proj, proj)


def _hgrn_kernel(layer, lbl_ref, ng_ref, q_ref, f_ref, i_ref, gate_ref, o_ref, state_scr):
    nc = q_ref.shape[1] // HG_CHUNK
    group_chunks = math.gcd(nc, HG_GROUP)
    logits = lbl_ref[...]
    e = jnp.exp(logits - jnp.max(logits, axis=0, keepdims=True))
    sm = e / jnp.sum(e, axis=0, keepdims=True)
    lb = jnp.sum(sm[1:layer + 1], axis=0, keepdims=True)
    ti = lax.broadcasted_iota(jnp.int32, (HG_CHUNK, HG_CHUNK), 0)
    si = lax.broadcasted_iota(jnp.int32, (HG_CHUNK, HG_CHUNK), 1)
    cumsum_mat = jnp.where(si <= ti, 1.0, 0.0).astype(BF16)
    row8 = lax.broadcasted_iota(jnp.int32, (SUBLANES, 1), 0)
    n_sub = HG_CHUNK // HG_SUB
    state_scr[...] = jnp.zeros_like(state_scr)

    def pair_rows(acc, qt, gt, ks, gs, vs, s):
        decay = jnp.exp2(gt - gs[s:s + 1, :])
        col = jnp.sum(qt * (ks[s:s + 1, :] * decay), axis=-1, keepdims=True)
        col = jnp.where(row8 >= s, col, 0.0)
        return acc + col * vs[s:s + 1, :]

    def chunk(c, _):
        rows = pl.ds(pl.multiple_of(c * HG_CHUNK, HG_CHUNK), HG_CHUNK)
        z = f_ref[0, rows, :]
        qx = q_ref[0, rows, :].astype(F32)
        v_all = i_ref[0, rows, :].astype(F32)
        gate = gate_ref[0, rows, :].astype(F32)
        sig = _sigmoid(z)
        q_all = qx * _sigmoid(qx)
        k_all = (1.0 - lb) * (1.0 - sig)
        log2_f = jnp.log2(lb + (1.0 - lb) * sig)
        g_all = _split_dot_left(cumsum_mat, log2_f)
        g_last_all = g_all[HG_CHUNK - 1:HG_CHUNK, :]
        gated = gate * _sigmoid(gate)
        heads = range(HG_HEADS)
        cols = [slice(h * HG_DK, (h + 1) * HG_DK) for h in heads]
        v_bf = v_all.astype(BF16)
        state = [state_scr[h] for h in heads]
        qg_all = (q_all * jnp.exp2(g_all)).astype(BF16)
        kd_all = (k_all * jnp.exp2(g_last_all - g_all)).astype(BF16)
        o_inter = [_dot_nt(qg_all[:, cols[h]], state[h].astype(BF16)) for h in heads]
        new_state = [state[h] * jnp.exp2(g_last_all[:, cols[h]]) + _dot_tn(v_bf[:, cols[h]], kd_all[:, cols[h]])
                     for h in heads]
        kt_all, qt_all = [], []
        for j in range(n_sub - 1):
            lo, hi = j * HG_SUB, (j + 1) * HG_SUB
            r = g_all[hi - 1:hi, :]
            kt_all.append((k_all[lo:hi] * jnp.exp2(r - g_all[lo:hi])).astype(BF16))
            qt_all.append((q_all[hi:] * jnp.exp2(g_all[hi:] - r)).astype(BF16))
        q_low, k_up, v_up = [], [], []
        for i in range(n_sub):
            lo, mid, hi = i * HG_SUB, i * HG_SUB + SUBLANES, (i + 1) * HG_SUB
            r = g_all[mid - 1:mid, :]
            q_low.append(q_all[mid:hi] * jnp.exp2(g_all[mid:hi] - r))
            k_up.append(k_all[lo:mid] * jnp.exp2(r - g_all[lo:mid]))
            v_up.append(v_all[lo:mid])
        q_low = jnp.concatenate(q_low, axis=0).astype(BF16)
        k_up = jnp.concatenate(k_up, axis=0).astype(BF16)
        v_up = jnp.concatenate(v_up, axis=0).astype(BF16)
        same_sub = (lax.broadcasted_iota(jnp.int32, (n_sub * SUBLANES, n_sub * SUBLANES), 0) // SUBLANES
                    == lax.broadcasted_iota(jnp.int32, (n_sub * SUBLANES, n_sub * SUBLANES), 1) // SUBLANES)
        sc = [[_dot_nt(qt_all[j][:, cols[h]], kt_all[j][:, cols[h]]).astype(BF16)
               for j in range(n_sub - 1)] for h in heads]
        contrib = [[_dot(sc[h][j], v_bf[j * HG_SUB:(j + 1) * HG_SUB, cols[h]])
                    for j in range(n_sub - 1)] for h in heads]
        sc_low = [jnp.where(same_sub, _dot_nt(q_low[:, cols[h]], k_up[:, cols[h]]), 0.0).astype(BF16) for h in heads]
        contrib_low = [_dot(sc_low[h], v_up[:, cols[h]]) for h in heads]
        tiles = [[None] * (HG_CHUNK // SUBLANES) for _ in heads]
        for h in heads:
            for i in range(n_sub):
                part = jnp.zeros((HG_SUB, HG_DK), F32)
                for j in range(i):
                    part = part + contrib[h][j][(i - j - 1) * HG_SUB:(i - j) * HG_SUB]
                tiles[h][2 * i] = part[:SUBLANES]
                tiles[h][2 * i + 1] = part[SUBLANES:] + contrib_low[h][i * SUBLANES:(i + 1) * SUBLANES]
        for i in range(2 * n_sub):
            a = slice(i * SUBLANES, (i + 1) * SUBLANES)
            for s in range(SUBLANES):
                for h in heads:
                    g, q, k, v = g_all[:, cols[h]], q_all[:, cols[h]], k_all[:, cols[h]], v_all[:, cols[h]]
                    tiles[h][i] = pair_rows(tiles[h][i], q[a], g[a], k[a], g[a], v[a], s)
        for h in heads:
            o = o_inter[h] + jnp.concatenate(tiles[h], axis=0)
            state_scr[h] = new_state[h]
            o_ref[0, rows, cols[h]] = (_rms(o, ng_ref[...]) * gated[:, cols[h]]).astype(o_ref.dtype)
        return 0

    def chunk_group(gi, _):
        n_rows = group_chunks * HG_CHUNK
        rows = pl.ds(pl.multiple_of(gi * n_rows, n_rows), n_rows)
        part = [slice(cc * HG_CHUNK, (cc + 1) * HG_CHUNK) for cc in range(group_chunks)]
        heads = range(HG_HEADS)
        cols = [slice(h * HG_DK, (h + 1) * HG_DK) for h in heads]
        pairs = [(cc, h) for cc in range(group_chunks) for h in heads]
        z = f_ref[0, rows, :]
        qx = q_ref[0, rows, :].astype(F32)
        v_bf = i_ref[0, rows, :]
        gate = gate_ref[0, rows, :].astype(F32)
        sig = _sigmoid(z)
        q_all = qx * _sigmoid(qx)
        k_all = (1.0 - lb) * (1.0 - sig)
        log2_f = jnp.log2(lb + (1.0 - lb) * sig)
        gated = gate * _sigmoid(gate)
        hi = log2_f.astype(BF16)
        lo = (log2_f - hi.astype(F32)).astype(BF16)
        g = [_dot(cumsum_mat, hi[p]) + _dot(cumsum_mat, lo[p]) for p in part]
        r = [0.5 * gc[HG_CHUNK - 1:HG_CHUNK, :] for gc in g]
        half = [jnp.exp2(rc) for rc in r]
        centred = jnp.concatenate([gc - rc for gc, rc in zip(g, r)], axis=0)
        qs = q_all * jnp.exp2(centred)
        ks = k_all * jnp.exp2(-centred)
        qs_bf, ks_bf = qs.astype(BF16), ks.astype(BF16)
        qg = [(qs[p] * hc).astype(BF16) for p, hc in zip(part, half)]
        kd = [(ks[p] * hc).astype(BF16) for p, hc in zip(part, half)]
        update = {(cc, h): _dot_tn(v_bf[part[cc], cols[h]], kd[cc][:, cols[h]]) for cc, h in pairs}
        sc = {(cc, h): jnp.where(si <= ti, _dot_nt(qs_bf[part[cc], cols[h]], ks_bf[part[cc], cols[h]]), 0.0
                                 ).astype(BF16) for cc, h in pairs}
        intra = {(cc, h): _dot(sc[cc, h], v_bf[part[cc], cols[h]]) for cc, h in pairs}
        state = [state_scr[h] for h in heads]
        inter = {}
        for cc in range(group_chunks):
            for h in heads:
                inter[cc, h] = _dot_nt(qg[cc][:, cols[h]], state[h].astype(BF16))
                state[h] = state[h] * (half[cc][:, cols[h]] * half[cc][:, cols[h]]) + update[cc, h]
        for h in heads:
            state_scr[h] = state[h]
        for cc, h in pairs:
            out_rows = pl.ds(pl.multiple_of(gi * n_rows + cc * HG_CHUNK, HG_CHUNK), HG_CHUNK)
            o_ref[0, out_rows, cols[h]] = (_rms(inter[cc, h] + intra[cc, h], ng_ref[...]) * gated[part[cc], cols[h]]
                                           ).astype(o_ref.dtype)
        return 0

    def chunk_decay(c, worst):
        rows = pl.ds(pl.multiple_of(c * HG_CHUNK, HG_CHUNK), HG_CHUNK)
        log2_f = jnp.log2(lb + (1.0 - lb) * _sigmoid(f_ref[0, rows, :]))
        return jnp.minimum(worst, jnp.sum(log2_f, axis=0, keepdims=True))

    worst = jnp.min(lax.fori_loop(0, nc, chunk_decay, jnp.zeros((1, HG_WIDTH), F32), unroll=HG_UNROLL))
    gentle = worst >= -HG_SAFE_DECAY

    @pl.when(gentle)
    def _():
        lax.fori_loop(0, nc // group_chunks, chunk_group, 0)

    @pl.when(jnp.logical_not(gentle))
    def _():
        lax.fori_loop(0, nc, chunk, 0, unroll=HG_UNROLL)


def _hgrn(proj, f_logit, lb_logits, norm_g, layer):
    b, s, _ = proj.shape
    depth = lb_logits.shape[0]
    assert HG_SUB == 2 * SUBLANES
    blk = lambda off: pl.BlockSpec((1, s, HG_WIDTH), lambda i: (i, 0, off))
    return pl.pallas_call(
        functools.partial(_hgrn_kernel, layer),
        grid=(b,),
        in_specs=[
            pl.BlockSpec((depth, HG_WIDTH), lambda i: (0, 0)),
            pl.BlockSpec((1, HG_DK), lambda i: (0, 0)),
            blk(0), blk(0), blk(2), blk(3),
        ],
        out_specs=pl.BlockSpec((1, s, HG_WIDTH), lambda i: (i, 0, 0)),
        out_shape=jax.ShapeDtypeStruct((b, s, HG_WIDTH), BF16),
        scratch_shapes=[pltpu.VMEM((HG_HEADS, HG_DK, HG_DK), F32)],
        compiler_params=_params(("parallel",)),
        name="hgrn2",
    )(lb_logits, norm_g.reshape(1, HG_DK), proj, f_logit, proj, proj)


def _attn_columns(w_in):
    split = SWA_Q + 2 * SWA_KV
    return jnp.concatenate([w_in[:, split:], w_in[:, :split]], axis=1)


def kernel(x, norm_g, ffn_w_in, ffn_w_out, attn_w_in, attn_sinks, attn_w_out, rec_w_in,
           hgrn_lb_logits, hgrn_norm_g, conv_w, rec_w_out, final_g):
    b, s, d = x.shape
    depth = norm_g.shape[0]
    m = b * s
    h = x.reshape(m, d)
    w_in_bf, w_out_bf = ffn_w_in.astype(BF16), ffn_w_out.astype(BF16)
    for layer in range(depth):
        h = _ffn(h, norm_g[layer, 0], w_in_bf, w_out_bf, layer, 0, final_g, final_norm=False)
        if layer % 2 == 0:
            e = layer // 2
            proj = _proj_in(h, norm_g[layer, 1], _attn_columns(attn_w_in[e]).astype(BF16)).reshape(b, s, ATTN_IN)
            o_a = _swa(proj, attn_sinks[e])
            o_b = _sb(proj)
            mix = (o_a.reshape(m, SWA_Q), o_b.reshape(m, SB_W), attn_w_out[e].astype(BF16))
        else:
            o = layer // 2
            proj, f_logit, o_d = _proj_rec(h, norm_g[layer, 1], rec_w_in[o].astype(BF16), conv_w[o], s)
            o_c = _hgrn(proj.reshape(b, s, 4 * HG_WIDTH), f_logit.reshape(b, s, HG_WIDTH), hgrn_lb_logits,
                        hgrn_norm_g[o], layer)
            mix = (o_c.reshape(m, HG_WIDTH), o_d, rec_w_out[o].astype(BF16))
        h = _ffn(h, norm_g[layer, 2], w_in_bf, w_out_bf, layer, 1, final_g,
                 final_norm=(layer == depth - 1), mix=mix)
    return h.reshape(b, s, d)
```
